```python
import math
import jax, jax.numpy as jnp
from jax import lax
import numpy as np

D_MODEL = 2048
BATCH = 2
SEQ = 4096
DEPTH = 2
DEC_BATCH = 16
DEC_SEQ = 32
PAST_LEN = 4096

CHUNK = 64
Q_BLOCK = 128
N_ATT_LAYERS = (DEPTH + 1) // 2
N_POOL_LAYERS = DEPTH // 2
H_A = 8
DQK = 64
DV_A = 2 * DQK
W_A = H_A * DV_A
H_B = 8
Q_LORA = 512
KV_LORA = 512
NOPE_DIM = 128
ROPE_DIM = 64
V_DIM = 128
W_B = H_B * V_DIM
ROPE_THETA = 10000.0
W_ATT = W_A + W_B
ATT_SPLITS = [W_A, 2 * W_A, 3 * W_A, 3 * W_A + Q_LORA, 3 * W_A + Q_LORA + KV_LORA,
              3 * W_A + Q_LORA + KV_LORA + ROPE_DIM]
IN_ATT = 3 * W_A + Q_LORA + KV_LORA + ROPE_DIM + W_ATT
POOL_WINDOWS = (2, 4, 8, 16)
N_POOL_GROUPS = 4
W_C = D_MODEL
POOL_GW = W_C // N_POOL_GROUPS
POOL_STATE = max(POOL_WINDOWS) - 1
EPS = 1e-6
SUBLN_EPS = 1e-5
NEG_INF = -1e30
DIFF_SCALE = DQK ** -0.5
MLA_SCALE = (NOPE_DIM + ROPE_DIM) ** -0.5

kernel_name = "chunk_causal_diffattn_mla_pool_stream_step"


def _rmsnorm(x, g, eps=EPS):
    xf = x.astype(jnp.float32)
    y = xf * lax.rsqrt(jnp.mean(xf * xf, axis=-1, keepdims=True) + eps)
    return (y * g.astype(jnp.float32)).astype(x.dtype)


def _rope(x, pos):
    half = ROPE_DIM // 2
    inv = ROPE_THETA ** (-jnp.arange(half, dtype=jnp.float32) / half)
    ang = pos.astype(jnp.float32)[:, None] * inv[None, :]
    cos = jnp.cos(ang)[None, :, None, :]
    sin = jnp.sin(ang)[None, :, None, :]
    x1 = x[..., :half].astype(jnp.float32)
    x2 = x[..., half:].astype(jnp.float32)
    return jnp.concatenate([x1 * cos - x2 * sin, x1 * sin + x2 * cos], axis=-1).astype(x.dtype)


def _lambda_init(layer_idx):
    return 0.8 - 0.6 * math.exp(-0.3 * layer_idx)


def _masked_softmax(logits, q_chunk, k_chunk):
    mask = k_chunk[None, :] <= q_chunk[:, None]
    return jax.nn.softmax(jnp.where(mask, logits, NEG_INF), axis=-1)


def _sweep_query_blocks(fn, q_chunk, qs):
    sq = q_chunk.shape[0]
    if sq <= Q_BLOCK:
        return fn(q_chunk, qs)
    nb = sq // Q_BLOCK
    qc = q_chunk.reshape(nb, Q_BLOCK)
    qb = tuple(jnp.swapaxes(q.reshape(q.shape[0], nb, Q_BLOCK, *q.shape[2:]), 0, 1) for q in qs)
    out = lax.map(lambda a: fn(a[0], a[1]), (qc, qb))
    out = jnp.swapaxes(out, 0, 1)
    return out.reshape(out.shape[0], sq, *out.shape[3:])


def _diff_attention(q, k, v, lam, q_chunk, k_chunk):
    def blk(qc, qs):
        (qb,) = qs
        logits = jnp.einsum('bqhnd,bkhnd->bnhqk', qb, k,
                            preferred_element_type=jnp.float32) * DIFF_SCALE
        p = _masked_softmax(logits, qc, k_chunk)
        w = p[:, 0] - lam * p[:, 1]
        return jnp.einsum('bhqk,bkhe->bqhe', w.astype(v.dtype), v)
    return _sweep_query_blocks(blk, q_chunk, (q,))


def _mla_attention(q_lat, q_pe, c_kv, k_pe, q_chunk, k_chunk):
    def blk(qc, qs):
        ql, qp = qs
        logits = (jnp.einsum('bqhc,bkc->bhqk', ql, c_kv, preferred_element_type=jnp.float32)
                  + jnp.einsum('bqhr,bkr->bhqk', qp, k_pe, preferred_element_type=jnp.float32)) * MLA_SCALE
        p = _masked_softmax(logits, qc, k_chunk)
        return jnp.einsum('bhqk,bkc->bqhc', p.astype(c_kv.dtype), c_kv)
    return _sweep_query_blocks(blk, q_chunk, (q_lat, q_pe))


def _attn_mixer(h, pos, k_pos, lam_init, cache_k, cache_v, cache_ckv, cache_kpe,
                w_in, w_out, lam_q1, lam_k1, lam_q2, lam_k2, g_subln,
                g_qa, w_uq, w_uqr, w_uk, w_uv, g_kva):
    b, s, _ = h.shape
    z = h @ w_in
    qa, ka, va, cq, ckv, kpe, gate = jnp.split(z, ATT_SPLITS, axis=-1)
    qa = qa.reshape(b, s, H_A, 2, DQK)
    ka = ka.reshape(b, s, H_A, DV_A)
    va = va.reshape(b, s, H_A, DV_A)
    cq = _rmsnorm(cq, g_qa)
    q_nope = jnp.einsum('bsc,chd->bshd', cq, w_uq)
    q_pe = _rope(jnp.einsum('bsc,chr->bshr', cq, w_uqr), pos)
    q_lat = jnp.einsum('bshd,chd->bshc', q_nope, w_uk)
    ckv = _rmsnorm(ckv, g_kva)
    kpe = _rope(kpe[:, :, None, :], pos)[:, :, 0, :]
    new_rows = (ka, va, ckv, kpe)
    if cache_k is not None:
        k_all = jnp.concatenate([cache_k.astype(ka.dtype), ka], axis=1)
        v_all = jnp.concatenate([cache_v.astype(va.dtype), va], axis=1)
        ckv_all = jnp.concatenate([cache_ckv.astype(ckv.dtype), ckv], axis=1)
        kpe_all = jnp.concatenate([cache_kpe.astype(kpe.dtype), kpe], axis=1)
    else:
        k_all, v_all, ckv_all, kpe_all = ka, va, ckv, kpe
    sk = k_all.shape[1]
    q_chunk = pos // CHUNK
    k_chunk = k_pos // CHUNK
    lam = (jnp.exp(jnp.sum(lam_q1.astype(jnp.float32) * lam_k1.astype(jnp.float32)))
           - jnp.exp(jnp.sum(lam_q2.astype(jnp.float32) * lam_k2.astype(jnp.float32))) + lam_init)
    o_a = _diff_attention(qa, k_all.reshape(b, sk, H_A, 2, DQK), v_all, lam, q_chunk, k_chunk)
    o_a = _rmsnorm(o_a, g_subln, SUBLN_EPS) * (1.0 - lam_init)
    o_lat = _mla_attention(q_lat, q_pe, ckv_all, kpe_all, q_chunk, k_chunk)
    o_b = jnp.einsum('bshc,chv->bshv', o_lat, w_uv)
    o = jnp.concatenate([o_a.reshape(b, s, W_A), o_b.reshape(b, s, W_B)], axis=-1) * jax.nn.silu(gate)
    return o @ w_out, new_rows


def _pool_mixer(h, pos, prefix, w_in, w_grp, scale, w_out):
    b, s, _ = h.shape
    u, gate = jnp.split(h @ w_in, 2, axis=-1)
    u_ext = jnp.concatenate([prefix.astype(u.dtype), u], axis=1)
    p = POOL_STATE
    cs = jnp.cumsum(u_ext.astype(jnp.float32), axis=1)
    cs = jnp.pad(cs, ((0, 0), (1, 0), (0, 0)))
    end = cs[:, p + 1:p + 1 + s]
    pooled = []
    for g, w in enumerate(POOL_WINDOWS):
        sl = slice(g * POOL_GW, (g + 1) * POOL_GW)
        start = cs[:, p + 1 - w:p + 1 - w + s, sl]
        cnt = jnp.minimum(pos + 1, w).astype(jnp.float32)[None, :, None]
        pooled.append((end[..., sl] - start) / cnt)
    mean = jnp.concatenate(pooled, axis=-1)
    d = (mean - u.astype(jnp.float32)).astype(u.dtype).reshape(b, s, N_POOL_GROUPS, POOL_GW)
    m = jnp.einsum('bsgc,gcd->bsgd', d, w_grp).reshape(b, s, W_C) * scale
    y = (m * jax.nn.silu(gate)) @ w_out
    return y, u_ext[:, -POOL_STATE:]


def _trunk(x, c, pos, k_pos, cache_diff_k, cache_diff_v, cache_mla_ckv, cache_mla_kpe, state_pool,
           g_norm, w_ada, b_ada, w_in_att, w_out_att, lam_q1, lam_k1, lam_q2, lam_k2, g_subln,
           g_qa, w_uq, w_uqr, w_uk, w_uv, g_kva, w_in_pool, w_grp_pool, scale_pool, w_out_pool, g_final):
    b = x.shape[0]
    rows_k, rows_v, rows_ckv, rows_kpe, rows_pool = [], [], [], [], []
    cond = jax.nn.silu(c)
    has_cache = cache_diff_k is not None
    for i in range(DEPTH):
        j = i // 2
        mod = (cond @ w_ada[i] + b_ada[i])[:, None, :]
        shift, scale, gate = jnp.split(mod, 3, axis=-1)
        h = _rmsnorm(x, g_norm[i]) * (1 + scale) + shift
        if i % 2 == 0:
            y, (k_r, v_r, ckv_r, kpe_r) = _attn_mixer(
                h, pos, k_pos, _lambda_init(i),
                cache_diff_k[j] if has_cache else None, cache_diff_v[j] if has_cache else None,
                cache_mla_ckv[j] if has_cache else None, cache_mla_kpe[j] if has_cache else None,
                w_in_att[j], w_out_att[j], lam_q1[j], lam_k1[j], lam_q2[j], lam_k2[j], g_subln[j],
                g_qa[j], w_uq[j], w_uqr[j], w_uk[j], w_uv[j], g_kva[j])
            rows_k.append(k_r)
            rows_v.append(v_r)
            rows_ckv.append(ckv_r)
            rows_kpe.append(kpe_r)
        else:
            prefix = state_pool[j] if has_cache else jnp.zeros((b, POOL_STATE, W_C), x.dtype)
            y, st = _pool_mixer(h, pos, prefix, w_in_pool[j], w_grp_pool[j], scale_pool[j], w_out_pool[j])
            rows_pool.append(st)
        x = x + gate * y
    return (_rmsnorm(x, g_final), jnp.stack(rows_k), jnp.stack(rows_v), jnp.stack(rows_ckv),
            jnp.stack(rows_kpe), jnp.stack(rows_pool))


def setup_inputs(seed: int = 0) -> dict:
    key = jax.random.key(seed)
    ks = jax.random.split(key, 40)

    def nrm(k, shape, s):
        return s * jax.random.normal(k, shape, jnp.float32)

    d = D_MODEL
    return {
        "x_prompt": nrm(ks[0], (BATCH, SEQ, d), 1.0),
        "x_sample": nrm(ks[1], (DEC_BATCH, DEC_SEQ, d), 1.0),
        "c_prompt": nrm(ks[2], (BATCH, d), 1.0),
        "c_sample": nrm(ks[3], (DEC_BATCH, d), 1.0),
        "cache_diff_k": nrm(ks[4], (N_ATT_LAYERS, DEC_BATCH, PAST_LEN, H_A, DV_A), 1.0),
        "cache_diff_v": nrm(ks[5], (N_ATT_LAYERS, DEC_BATCH, PAST_LEN, H_A, DV_A), 1.0),
        "cache_mla_ckv": nrm(ks[6], (N_ATT_LAYERS, DEC_BATCH, PAST_LEN, KV_LORA), 1.0),
        "cache_mla_kpe": nrm(ks[7], (N_ATT_LAYERS, DEC_BATCH, PAST_LEN, ROPE_DIM), 1.0),
        "state_pool": nrm(ks[8], (N_POOL_LAYERS, DEC_BATCH, POOL_STATE, W_C), 1.0),
        "g_norm": 1.0 + nrm(ks[9], (DEPTH, d), 0.05),
        "w_ada": nrm(ks[10], (DEPTH, d, 3 * d), 0.2 * d ** -0.5),
        "b_ada": nrm(ks[11], (DEPTH, 3 * d), 0.02),
        "w_in_att": nrm(ks[12], (N_ATT_LAYERS, d, IN_ATT), d ** -0.5),
        "w_out_att": nrm(ks[13], (N_ATT_LAYERS, W_ATT, d), W_ATT ** -0.5),
        "lam_q1": nrm(ks[14], (N_ATT_LAYERS, DQK), 0.1),
        "lam_k1": nrm(ks[15], (N_ATT_LAYERS, DQK), 0.1),
        "lam_q2": nrm(ks[16], (N_ATT_LAYERS, DQK), 0.1),
        "lam_k2": nrm(ks[17], (N_ATT_LAYERS, DQK), 0.1),
        "g_subln": 1.0 + nrm(ks[18], (N_ATT_LAYERS, DV_A), 0.05),
        "g_qa": 1.0 + nrm(ks[19], (N_ATT_LAYERS, Q_LORA), 0.05),
        "w_uq": nrm(ks[20], (N_ATT_LAYERS, Q_LORA, H_B, NOPE_DIM), Q_LORA ** -0.5),
        "w_uqr": nrm(ks[21], (N_ATT_LAYERS, Q_LORA, H_B, ROPE_DIM), Q_LORA ** -0.5),
        "w_uk": nrm(ks[22], (N_ATT_LAYERS, KV_LORA, H_B, NOPE_DIM), KV_LORA ** -0.5),
        "w_uv": nrm(ks[23], (N_ATT_LAYERS, KV_LORA, H_B, V_DIM), KV_LORA ** -0.5),
        "g_kva": 1.0 + nrm(ks[24], (N_ATT_LAYERS, KV_LORA), 0.05),
        "w_in_pool": nrm(ks[25], (N_POOL_LAYERS, d, 2 * W_C), d ** -0.5),
        "w_grp_pool": nrm(ks[26], (N_POOL_LAYERS, N_POOL_GROUPS, POOL_GW, POOL_GW), POOL_GW ** -0.5),
        "scale_pool": 1.0 + nrm(ks[27], (N_POOL_LAYERS, W_C), 0.1),
        "w_out_pool": nrm(ks[28], (N_POOL_LAYERS, W_C, d), W_C ** -0.5),
        "g_final": 1.0 + nrm(ks[29], (d,), 0.05),
    }


def reference(x_prompt, x_sample, c_prompt, c_sample, cache_diff_k, cache_diff_v, cache_mla_ckv,
              cache_mla_kpe, state_pool, g_norm, w_ada, b_ada, w_in_att, w_out_att, lam_q1, lam_k1,
              lam_q2, lam_k2, g_subln, g_qa, w_uq, w_uqr, w_uk, w_uv, g_kva, w_in_pool, w_grp_pool,
              scale_pool, w_out_pool, g_final):
    weights = (g_norm, w_ada, b_ada, w_in_att, w_out_att, lam_q1, lam_k1, lam_q2, lam_k2, g_subln,
               g_qa, w_uq, w_uqr, w_uk, w_uv, g_kva, w_in_pool, w_grp_pool, scale_pool, w_out_pool, g_final)
    s_p = x_prompt.shape[1]
    pos_p = jnp.arange(s_p, dtype=jnp.int32)
    y_prompt, pk, pv, pckv, pkpe, ppool = _trunk(
        x_prompt, c_prompt, pos_p, pos_p, None, None, None, None, None, *weights)
    past = cache_diff_k.shape[2]
    n = x_sample.shape[1]
    pos_s = past + jnp.arange(n, dtype=jnp.int32)
    k_pos_s = jnp.arange(past + n, dtype=jnp.int32)
    y_sample, sk, sv, sckv, skpe, spool = _trunk(
        x_sample, c_sample, pos_s, k_pos_s, cache_diff_k, cache_diff_v, cache_mla_ckv, cache_mla_kpe,
        state_pool, *weights)
    return (y_prompt, y_sample, pk, pv, pckv, pkpe, ppool, sk, sv, sckv, skpe, spool)
```

```python
import functools
import math

import jax
import jax.numpy as jnp
from jax import lax
from jax.experimental import pallas as pl
from jax.experimental.pallas import tpu as pltpu

F32 = jnp.float32
BF16 = jnp.bfloat16

D_MODEL = 2048
CHUNK = 64
CHUNK_SHIFT = 6
H_A = 8
DQK = 64
DV_A = 2 * DQK
W_A = H_A * DV_A
H_B = 8
Q_LORA = 512
KV_LORA = 512
NOPE_DIM = 128
ROPE_DIM = 64
V_DIM = 128
W_B = H_B * V_DIM
ROPE_THETA = 10000.0
W_ATT = W_A + W_B
POOL_WINDOWS = (2, 4, 8, 16)
N_POOL_GROUPS = 4
W_C = D_MODEL
POOL_GW = W_C // N_POOL_GROUPS
POOL_STATE = max(POOL_WINDOWS) - 1
POOL_PREFIX = POOL_STATE + 1
EPS = 1e-6
SUBLN_EPS = 1e-5
NEG_INF = -1e30
LOG2E = math.log2(math.e)
DIFF_QSCALE = DQK ** -0.5 * LOG2E
MLA_QSCALE = (NOPE_DIM + ROPE_DIM) ** -0.5 * LOG2E
MOD_ROWS = 32
SAMPLE_TILE_SEQS = 8
VMEM_LIMIT = 56 * 1024 * 1024


def _lambda_init(layer_idx):
    return 0.8 - 0.6 * math.exp(-0.3 * layer_idx)


def _rms(x, eps):
    return x * lax.rsqrt(jnp.mean(x * x, axis=-1, keepdims=True) + eps)


def _silu(x):
    return x * (1.0 / (1.0 + jnp.exp(-x)))


def _dot(a, b):
    return jnp.dot(a, b, preferred_element_type=F32)


def _dot_nt(a, b):
    return lax.dot_general(a, b, (((1,), (1,)), ((), ())), preferred_element_type=F32)


def _resident(shape):
    nd = len(shape)
    return pl.BlockSpec(shape, lambda *_: (0,) * nd, pipeline_mode=pl.Buffered(1))


def _params(*sem):
    return pltpu.CompilerParams(dimension_semantics=sem, vmem_limit_bytes=VMEM_LIMIT)


def _ada_kernel(c_ref, w_ref, b_ref, o_ref):
    cond = _silu(c_ref[...]).astype(BF16)
    w = w_ref[0].astype(BF16)
    o_ref[0, 0] = _dot(cond, w) + b_ref[0, 0]


def _ada(c_all, w_ada, b_ada):
    depth = w_ada.shape[0]
    tn = 1024
    per = D_MODEL // tn
    return pl.pallas_call(
        _ada_kernel,
        grid=(depth, 3 * per),
        in_specs=[
            pl.BlockSpec((MOD_ROWS, D_MODEL), lambda l, j: (0, 0)),
            pl.BlockSpec((1, D_MODEL, tn), lambda l, j: (l, 0, j)),
            pl.BlockSpec((1, 1, 1, tn), lambda l, j: (l, j // per, 0, j % per)),
        ],
        out_specs=pl.BlockSpec((1, 1, MOD_ROWS, tn), lambda l, j: (l, j // per, 0, j % per)),
        out_shape=jax.ShapeDtypeStruct((depth, 3, MOD_ROWS, D_MODEL), F32),
        compiler_params=_params("arbitrary", "arbitrary"),
        name="ada",
    )(c_all, w_ada, b_ada.reshape(depth, 3, 1, D_MODEL))


def _prenorm(x, gn, scale, shift):
    return _rms(x, EPS) * gn * (1.0 + scale) + shift


def _in_qkv_kernel(x_ref, shift_ref, scale_ref, gn_ref, wqkv_ref, wgate_ref,
                   qa_ref, k32_ref, v32_ref, k16_ref, v16_ref, sg_ref):
    nb, rows, _ = x_ref.shape
    h = _prenorm(x_ref[...], gn_ref[...], scale_ref[...], shift_ref[...])
    hb = h.reshape(nb * rows, D_MODEL).astype(BF16)
    z = _dot(hb, wqkv_ref[...])
    qa_ref[...] = (z[:, :W_A] * DIFF_QSCALE).astype(BF16)
    ka = z[:, W_A:2 * W_A]
    va = z[:, 2 * W_A:]
    k32_ref[...] = ka
    v32_ref[...] = va
    k16_ref[...] = ka.astype(BF16)
    v16_ref[...] = va.astype(BF16)
    sg_ref[...] = _silu(_dot(hb, wgate_ref[...])).astype(BF16)


def _latent_common(x_ref, shift_ref, scale_ref, gn_ref, cos_ref, sin_ref, wc_ref, wkpe_ref,
                   gqa_ref, gkva_ref, wuq_ref, wuqr_ref, wuqrr_ref, ckv32_ref, kpe32_ref):
    nb, rows, _ = x_ref.shape
    h = _prenorm(x_ref[...], gn_ref[...], scale_ref[...], shift_ref[...])
    hb = h.reshape(nb * rows, D_MODEL).astype(BF16)
    zc = _dot(hb, wc_ref[...])
    cq = (_rms(zc[:, :Q_LORA], EPS) * gqa_ref[...]).astype(BF16)
    ckv = _rms(zc[:, Q_LORA:], EPS) * gkva_ref[...]
    ckv32_ref[...] = ckv
    cos = cos_ref[...]
    sin = sin_ref[...]
    zk = _dot(hb, wkpe_ref[...])
    kpe_pad = zk[:, :128] * cos + zk[:, 128:] * sin
    kpe32_ref[...] = kpe_pad[:, :ROPE_DIM]
    q_nope = _dot(cq, wuq_ref[...]) * MLA_QSCALE
    cos8 = jnp.tile(cos, (1, H_B))
    sin8 = jnp.tile(sin, (1, H_B))
    q_pe = (_dot(cq, wuqr_ref[...]) * cos8 + _dot(cq, wuqrr_ref[...]) * sin8) * MLA_QSCALE
    return ckv, kpe_pad, q_nope, q_pe


def _in_lat_prompt_kernel(x_ref, shift_ref, scale_ref, gn_ref, cos_ref, sin_ref, wc_ref, wkpe_ref,
                          gqa_ref, gkva_ref, wuq_ref, wuqr_ref, wuqrr_ref, wuk_ref, wuv_ref,
                          ckv32_ref, kpe32_ref, qcat_ref, kcat_ref, vmla_ref):
    ckv, kpe_pad, q_nope, q_pe = _latent_common(
        x_ref, shift_ref, scale_ref, gn_ref, cos_ref, sin_ref, wc_ref, wkpe_ref,
        gqa_ref, gkva_ref, wuq_ref, wuqr_ref, wuqrr_ref, ckv32_ref, kpe32_ref)
    ckvb = ckv.astype(BF16)
    k_nope = _dot(ckvb, wuk_ref[...])
    vmla_ref[...] = _dot(ckvb, wuv_ref[...]).astype(BF16)
    kpe_b = kpe_pad.astype(BF16)
    for hd in range(H_B):
        sl = slice(hd * 128, (hd + 1) * 128)
        qcat_ref[:, hd * 256:hd * 256 + 128] = q_nope[:, sl].astype(BF16)
        qcat_ref[:, hd * 256 + 128:(hd + 1) * 256] = q_pe[:, sl].astype(BF16)
        kcat_ref[:, hd * 256:hd * 256 + 128] = k_nope[:, sl].astype(BF16)
        kcat_ref[:, hd * 256 + 128:(hd + 1) * 256] = kpe_b


def _in_lat_sample_kernel(x_ref, shift_ref, scale_ref, gn_ref, cos_ref, sin_ref, wc_ref, wkpe_ref,
                          gqa_ref, gkva_ref, wuq_ref, wuqr_ref, wuqrr_ref, wukt_ref,
                          ckv32_ref, kpe32_ref, qlat_ref, qpe_ref):
    _, _, q_nope, q_pe = _latent_common(
        x_ref, shift_ref, scale_ref, gn_ref, cos_ref, sin_ref, wc_ref, wkpe_ref,
        gqa_ref, gkva_ref, wuq_ref, wuqr_ref, wuqrr_ref, ckv32_ref, kpe32_ref)
    qn = q_nope.astype(BF16)
    for hd in range(H_B):
        qlat_ref[:, hd * KV_LORA:(hd + 1) * KV_LORA] = _dot(
            qn[:, hd * 128:(hd + 1) * 128], wukt_ref[hd]).astype(BF16)
    qpe_ref[...] = q_pe.astype(BF16)


def _online_softmax_step(s, v, m, l, acc):
    m_new = jnp.maximum(m, jnp.max(s, axis=1, keepdims=True))
    alpha = jnp.exp2(m - m_new)
    p = jnp.exp2(s - m_new)
    l = alpha * l + jnp.sum(p, axis=1, keepdims=True)
    acc = alpha * acc + _dot(p.astype(BF16), v)
    return m_new, l, acc


def _flash_rows(q, k_ref, v_ref, q_tile, tq, tk, pos_of_row):
    rows = q.shape[0]
    dv = v_ref.shape[2]
    n_diag = tq // tk

    def full_tile(j, carry):
        off = pl.multiple_of(j * tk, tk)
        s = _dot_nt(q, k_ref[0, pl.ds(off, tk), :])
        return _online_softmax_step(s, v_ref[0, pl.ds(off, tk), :], *carry)

    carry = (jnp.full((rows, 1), NEG_INF, F32), jnp.zeros((rows, 1), F32), jnp.zeros((rows, dv), F32))
    carry = lax.fori_loop(0, q_tile * n_diag, full_tile, carry)
    q_chunk = pos_of_row(lax.broadcasted_iota(jnp.int32, (rows, tk), 0)) >> CHUNK_SHIFT
    k_col = lax.broadcasted_iota(jnp.int32, (rows, tk), 1)
    for d in range(n_diag):
        off = pl.multiple_of(q_tile * tq + d * tk, tk)
        s = _dot_nt(q, k_ref[0, pl.ds(off, tk), :])
        s = jnp.where(((k_col + d * tk) >> CHUNK_SHIFT) <= q_chunk, s, NEG_INF)
        carry = _online_softmax_step(s, v_ref[0, pl.ds(off, tk), :], *carry)
    _, l, acc = carry
    return acc, l


def _lambda_full(lq1_ref, lk1_ref, lq2_ref, lk2_ref, lam_init):
    a = jnp.sum(lq1_ref[...] * lk1_ref[...], axis=-1, keepdims=True)
    b = jnp.sum(lq2_ref[...] * lk2_ref[...], axis=-1, keepdims=True)
    return jnp.exp(a) - jnp.exp(b) + lam_init


def _flash_diff_kernel(tq, tk, lam_init, q_ref, k_ref, v_ref, lq1_ref, lk1_ref, lq2_ref, lk2_ref,
                       gsub_ref, o_ref):
    q = q_ref[0]
    lane = lax.broadcasted_iota(jnp.int32, q.shape, 1)
    zero = jnp.zeros_like(q)
    q_bd = jnp.concatenate([jnp.where(lane < DQK, q, zero), jnp.where(lane >= DQK, q, zero)], axis=0)
    acc, l = _flash_rows(q_bd, k_ref, v_ref, pl.program_id(2), tq, tk,
                         lambda r: jnp.where(r >= tq, r - tq, r))
    o = acc / l
    lam = _lambda_full(lq1_ref, lk1_ref, lq2_ref, lk2_ref, lam_init)
    o = o[:tq] - lam * o[tq:]
    o_ref[0] = (_rms(o, SUBLN_EPS) * gsub_ref[...] * (1.0 - lam_init)).astype(o_ref.dtype)


def _flash_mla_kernel(tq, tk, q_ref, k_ref, v_ref, o_ref):
    acc, l = _flash_rows(q_ref[0], k_ref, v_ref, pl.program_id(2), tq, tk, lambda r: r)
    o_ref[0] = (acc / l).astype(o_ref.dtype)


def _flash_diff(qa, k16, v16, lam_refs, gsub, lam_init, b, s):
    tq = tk = 256
    vec = pl.BlockSpec((1, DQK), lambda bi, hi, qi: (0, 0))
    return pl.pallas_call(
        functools.partial(_flash_diff_kernel, tq, tk, lam_init),
        grid=(b, H_A, s // tq),
        in_specs=[
            pl.BlockSpec((1, tq, DV_A), lambda bi, hi, qi: (bi, qi, hi)),
            pl.BlockSpec((1, s, DV_A), lambda bi, hi, qi: (bi, 0, hi)),
            pl.BlockSpec((1, s, DV_A), lambda bi, hi, qi: (bi, 0, hi)),
            vec, vec, vec, vec,
            pl.BlockSpec((1, DV_A), lambda bi, hi, qi: (0, 0)),
        ],
        out_specs=pl.BlockSpec((1, tq, DV_A), lambda bi, hi, qi: (bi, qi, hi)),
        out_shape=jax.ShapeDtypeStruct((b, s, W_A), BF16),
        compiler_params=_params("arbitrary", "arbitrary", "arbitrary"),
        name="flash_diff",
    )(qa, k16, v16, *lam_refs, gsub)


def _flash_mla(qcat, kcat, vmla, b, s):
    tq, tk = 512, 256
    return pl.pallas_call(
        functools.partial(_flash_mla_kernel, tq, tk),
        grid=(b, H_B, s // tq),
        in_specs=[
            pl.BlockSpec((1, tq, 256), lambda bi, hi, qi: (bi, qi, hi)),
            pl.BlockSpec((1, s, 256), lambda bi, hi, qi: (bi, 0, hi)),
            pl.BlockSpec((1, s, V_DIM), lambda bi, hi, qi: (bi, 0, hi)),
        ],
        out_specs=pl.BlockSpec((1, tq, V_DIM), lambda bi, hi, qi: (bi, qi, hi)),
        out_shape=jax.ShapeDtypeStruct((b, s, W_B), BF16),
        compiler_params=_params("arbitrary", "arbitrary", "arbitrary"),
        name="flash_mla",
    )(qcat, kcat, vmla)


def _dec_update(s, v, m_ref, l_ref, acc_ref, g):
    m_new, l, acc = _online_softmax_step(s, v, m_ref[g], l_ref[g], acc_ref[g])
    m_ref[g] = m_new
    l_ref[g] = l
    acc_ref[g] = acc


def _dec_diff_kernel(n, lam_init, q_ref, k_ref, v_ref, kn_ref, vn_ref, lq1_ref, lk1_ref, lq2_ref,
                     lk2_ref, gsub_ref, o_ref, qbd_ref, m_ref, l_ref, acc_ref):
    j = pl.program_id(1)
    gw = 4 * DV_A

    @pl.when(j == 0)
    def _():
        q = q_ref[...]
        for g in range(2):
            qg = q[:, g * gw:(g + 1) * gw]
            lane = lax.broadcasted_iota(jnp.int32, qg.shape, 1)
            blocks = [jnp.where((lane >> CHUNK_SHIFT) == blk, qg, jnp.zeros_like(qg))
                      for blk in range(8)]
            qbd_ref[g] = jnp.concatenate(blocks, axis=0)
        m_ref[...] = jnp.full(m_ref.shape, NEG_INF, F32)
        l_ref[...] = jnp.zeros(l_ref.shape, F32)
        acc_ref[...] = jnp.zeros(acc_ref.shape, F32)

    for g in range(2):
        kt = k_ref[0, :, g * gw:(g + 1) * gw].astype(BF16)
        vt = v_ref[0, :, g * gw:(g + 1) * gw].astype(BF16)
        _dec_update(_dot_nt(qbd_ref[g], kt), vt, m_ref, l_ref, acc_ref, g)

    @pl.when(j == pl.num_programs(1) - 1)
    def _():
        lam = _lambda_full(lq1_ref, lk1_ref, lq2_ref, lk2_ref, lam_init)
        for g in range(2):
            kt = kn_ref[:, g * gw:(g + 1) * gw].astype(BF16)
            vt = vn_ref[:, g * gw:(g + 1) * gw].astype(BF16)
            _dec_update(_dot_nt(qbd_ref[g], kt), vt, m_ref, l_ref, acc_ref, g)
            o = acc_ref[g] / l_ref[g]
            for hl in range(4):
                cols = slice(hl * DV_A, (hl + 1) * DV_A)
                o0 = o[(2 * hl) * n:(2 * hl + 1) * n, cols]
                o1 = o[(2 * hl + 1) * n:(2 * hl + 2) * n, cols]
                od = _rms(o0 - lam * o1, SUBLN_EPS) * gsub_ref[...] * (1.0 - lam_init)
                hd = g * 4 + hl
                o_ref[:, hd * DV_A:(hd + 1) * DV_A] = od.astype(o_ref.dtype)


def _dec_diff(qa, cache_k, cache_v, k_new, v_new, lam_refs, gsub, lam_init, b, n):
    past = cache_k.shape[1]
    tk = 1024
    vec = pl.BlockSpec((1, DQK), lambda bi, j: (0, 0))
    rows = pl.BlockSpec((n, W_A), lambda bi, j: (bi, 0))
    return pl.pallas_call(
        functools.partial(_dec_diff_kernel, n, lam_init),
        grid=(b, past // tk),
        in_specs=[
            rows,
            pl.BlockSpec((1, tk, W_A), lambda bi, j: (bi, j, 0)),
            pl.BlockSpec((1, tk, W_A), lambda bi, j: (bi, j, 0)),
            rows, rows, vec, vec, vec, vec,
            pl.BlockSpec((1, DV_A), lambda bi, j: (0, 0)),
        ],
        out_specs=rows,
        out_shape=jax.ShapeDtypeStruct((b * n, W_A), BF16),
        scratch_shapes=[
            pltpu.VMEM((2, 8 * n, 4 * DV_A), BF16),
            pltpu.VMEM((2, 8 * n, 1), F32),
            pltpu.VMEM((2, 8 * n, 1), F32),
            pltpu.VMEM((2, 8 * n, 4 * DV_A), F32),
        ],
        compiler_params=_params("arbitrary", "arbitrary"),
        name="dec_diff",
    )(qa, cache_k, cache_v, k_new, v_new, *lam_refs, gsub)


def _dec_mla_kernel(n, qlat_ref, qpe_ref, ckv_ref, kpe_ref, ckvn_ref, kpen_ref, wuv_ref, o_ref,
                    ql_ref, qp_ref, m_ref, l_ref, acc_ref):
    j = pl.program_id(1)

    @pl.when(j == 0)
    def _():
        for hd in range(H_B):
            ql_ref[hd * n:(hd + 1) * n, :] = qlat_ref[:, hd * KV_LORA:(hd + 1) * KV_LORA]
            qp_ref[hd * n:(hd + 1) * n, :] = qpe_ref[:, hd * 128:(hd + 1) * 128]
        m_ref[...] = jnp.full(m_ref.shape, NEG_INF, F32)
        l_ref[...] = jnp.zeros(l_ref.shape, F32)
        acc_ref[...] = jnp.zeros(acc_ref.shape, F32)

    def update(ckv, kpe):
        ckv = ckv.astype(BF16)
        s = _dot_nt(ql_ref[...], ckv) + _dot_nt(qp_ref[:, :ROPE_DIM], kpe.astype(BF16))
        _dec_update(s, ckv, m_ref, l_ref, acc_ref, 0)

    update(ckv_ref[0], kpe_ref[0])

    @pl.when(j == pl.num_programs(1) - 1)
    def _():
        update(ckvn_ref[...], kpen_ref[...])
        o = (acc_ref[0] / l_ref[0]).astype(BF16)
        for hd in range(H_B):
            o_ref[:, hd * V_DIM:(hd + 1) * V_DIM] = _dot(
                o[hd * n:(hd + 1) * n], wuv_ref[hd]).astype(o_ref.dtype)


def _dec_mla(qlat, qpe, cache_ckv, cache_kpe, ckv_new, kpe_new, wuv_h, b, n):
    past = cache_ckv.shape[1]
    tk = 1024
    return pl.pallas_call(
        functools.partial(_dec_mla_kernel, n),
        grid=(b, past // tk),
        in_specs=[
            pl.BlockSpec((n, H_B * KV_LORA), lambda bi, j: (bi, 0)),
            pl.BlockSpec((n, H_B * 128), lambda bi, j: (bi, 0)),
            pl.BlockSpec((1, tk, KV_LORA), lambda bi, j: (bi, j, 0)),
            pl.BlockSpec((1, tk, ROPE_DIM), lambda bi, j: (bi, j, 0)),
            pl.BlockSpec((n, KV_LORA), lambda bi, j: (bi, 0)),
            pl.BlockSpec((n, ROPE_DIM), lambda bi, j: (bi, 0)),
            pl.BlockSpec((H_B, KV_LORA, V_DIM), lambda bi, j: (0, 0, 0)),
        ],
        out_specs=pl.BlockSpec((n, W_B), lambda bi, j: (bi, 0)),
        out_shape=jax.ShapeDtypeStruct((b * n, W_B), BF16),
        scratch_shapes=[
            pltpu.VMEM((H_B * n, KV_LORA), BF16),
            pltpu.VMEM((H_B * n, 128), BF16),
            pltpu.VMEM((1, H_B * n, 1), F32),
            pltpu.VMEM((1, H_B * n, 1), F32),
            pltpu.VMEM((1, H_B * n, KV_LORA), F32),
        ],
        compiler_params=_params("arbitrary", "arbitrary"),
        name="dec_mla",
    )(qlat, qpe, cache_ckv, cache_kpe, ckv_new, kpe_new, wuv_h)


def _out_proj_kernel(oa_ref, ob_ref, sg_ref, x_ref, gate_ref, w_ref, o_ref):
    nb, rows, _ = x_ref.shape
    sg = sg_ref[...].astype(F32)
    oa = (oa_ref[...].astype(F32) * sg[:, :W_A]).astype(BF16)
    ob = (ob_ref[...].astype(F32) * sg[:, W_A:]).astype(BF16)
    y = _dot(oa, w_ref[:W_A, :]) + _dot(ob, w_ref[W_A:, :])
    o_ref[...] = x_ref[...] + gate_ref[...] * y.reshape(nb, rows, D_MODEL)


def _pool_kernel(tiles_per_seq, pos0, carry, x_ref, shift_ref, scale_ref, gate_ref, gn_ref, pre_ref,
                 win_ref, wgrp_ref, sp_ref, wout_ref, gf_ref, y_ref, st_ref, uext_ref):
    nb, rows, _ = x_ref.shape
    tm = nb * rows
    t = pl.program_id(0) % tiles_per_seq
    x = x_ref[...]
    h = _prenorm(x, gn_ref[...], scale_ref[...], shift_ref[...])
    z = _dot(h.reshape(tm, D_MODEL).astype(BF16), win_ref[...])
    u = z[:, :W_C].reshape(nb, rows, W_C)
    sgate = _silu(z[:, W_C:])

    if carry:
        @pl.when(t == 0)
        def _():
            uext_ref[:, :POOL_PREFIX, :] = pre_ref[...]
    else:
        uext_ref[:, :POOL_PREFIX, :] = pre_ref[...]
    uext_ref[:, POOL_PREFIX:, :] = u

    pos = pos0 + t * rows + lax.broadcasted_iota(jnp.int32, (1, rows, 1), 1)
    parts = []
    for g, w in enumerate(POOL_WINDOWS):
        cols = slice(g * POOL_GW, (g + 1) * POOL_GW)
        ws = u[:, :, cols]
        for k in range(1, w):
            ws = ws + uext_ref[:, POOL_PREFIX - k:POOL_PREFIX - k + rows, cols]
        cnt = jnp.minimum(pos + 1, w).astype(F32)
        dgrp = (ws / cnt - u[:, :, cols]).reshape(tm, POOL_GW).astype(BF16)
        parts.append(_dot(dgrp, wgrp_ref[g]))
    m = jnp.concatenate(parts, axis=1) * sp_ref[...]
    y = _dot((m * sgate).astype(BF16), wout_ref[...])
    x2 = x + gate_ref[...] * y.reshape(nb, rows, D_MODEL)
    y_ref[...] = _rms(x2, EPS) * gf_ref[...]
    tail = uext_ref[:, rows:rows + POOL_PREFIX, :]
    st_ref[...] = tail
    if carry:
        uext_ref[:, :POOL_PREFIX, :] = tail


def _rot_half_cols(w):
    half = ROPE_DIM // 2
    return jnp.concatenate([-w[..., half:], w[..., :half]], axis=-1)


def _rope_tables(pos):
    half = ROPE_DIM // 2
    inv = ROPE_THETA ** (-jnp.arange(half, dtype=F32) / half)
    ang = pos.astype(F32)[:, None] * inv[None, :]
    cos, sin = jnp.cos(ang), jnp.sin(ang)
    zero = jnp.zeros((pos.shape[0], 128 - ROPE_DIM), F32)
    return jnp.concatenate([cos, cos, zero], axis=1), jnp.concatenate([sin, sin, zero], axis=1)


def _group(x3, mod, cos, sin, sample, cache, w, tm_att, tm_pool):
    bt, st, _ = x3.shape
    tokens = bt * st
    lam_init = _lambda_init(0)

    if sample:
        nb, rows, tps = SAMPLE_TILE_SEQS, st, 1
        n_tiles = bt // nb
        x_spec = pl.BlockSpec((nb, rows, D_MODEL), lambda i: (i, 0, 0))

        def mod_spec(layer, comp):
            blk = (layer * 3 + comp) * MOD_ROWS // nb
            return pl.BlockSpec((nb, 1, D_MODEL), lambda i: (blk + i, 0, 0))
    else:
        nb, rows = 1, tm_att
        tps = st // rows
        n_tiles = bt * tps
        x_spec = pl.BlockSpec((1, rows, D_MODEL), lambda i: (i // tps, i % tps, 0))

        def mod_spec(layer, comp):
            base = (layer * 3 + comp) * MOD_ROWS + 16
            return pl.BlockSpec((1, 1, D_MODEL), lambda i: (base + i // tps, 0, 0))

    tm = nb * rows

    def tok(width):
        return pl.BlockSpec((tm, width), lambda i: (i, 0))

    def tok_shape(width, dt):
        return jax.ShapeDtypeStruct((tokens, width), dt)

    vec = lambda width: pl.BlockSpec((1, width), lambda i: (0, 0))
    if sample:
        tab = pl.BlockSpec((tm, 128), lambda i: (0, 0))
    else:
        tab = pl.BlockSpec((tm, 128), lambda i: (i % tps, 0))

    qa, k32, v32, k16, v16, sgate = pl.pallas_call(
        _in_qkv_kernel,
        grid=(n_tiles,),
        in_specs=[x_spec, mod_spec(0, 0), mod_spec(0, 1), vec(D_MODEL),
                  _resident((D_MODEL, 3 * W_A)), _resident((D_MODEL, W_ATT))],
        out_specs=[tok(W_A), tok(W_A), tok(W_A), tok(W_A), tok(W_A), tok(W_ATT)],
        out_shape=[tok_shape(W_A, BF16), tok_shape(W_A, F32), tok_shape(W_A, F32),
                   tok_shape(W_A, BF16), tok_shape(W_A, BF16), tok_shape(W_ATT, BF16)],
        compiler_params=_params("arbitrary"),
        name="in_qkv",
    )(x3, mod, mod, w["g_norm0"], w["wqkv"], w["wgate"])

    lat_in = [x_spec, mod_spec(0, 0), mod_spec(0, 1), vec(D_MODEL), tab, tab,
              _resident((D_MODEL, Q_LORA + KV_LORA)), _resident((D_MODEL, 256)),
              vec(Q_LORA), vec(KV_LORA),
              _resident((Q_LORA, H_B * 128)), _resident((Q_LORA, H_B * 128)), _resident((Q_LORA, H_B * 128))]
    lat_args = [x3, mod, mod, w["g_norm0"], cos, sin, w["wc"], w["wkpe"], w["g_qa"], w["g_kva"],
                w["wuq"], w["wuqr"], w["wuqrr"]]
    lam_refs = (w["lam_q1"], w["lam_k1"], w["lam_q2"], w["lam_k2"])

    if sample:
        ckv32, kpe32, qlat, qpe = pl.pallas_call(
            _in_lat_sample_kernel,
            grid=(n_tiles,),
            in_specs=lat_in + [_resident((H_B, 128, KV_LORA))],
            out_specs=[tok(KV_LORA), tok(ROPE_DIM), tok(H_B * KV_LORA), tok(H_B * 128)],
            out_shape=[tok_shape(KV_LORA, F32), tok_shape(ROPE_DIM, F32),
                       tok_shape(H_B * KV_LORA, BF16), tok_shape(H_B * 128, BF16)],
            compiler_params=_params("arbitrary"),
            name="in_lat_sample",
        )(*lat_args, w["wukt"])
        cache_k, cache_v, cache_ckv, cache_kpe = cache
        o_a = _dec_diff(qa, cache_k, cache_v, k32, v32, lam_refs, w["g_subln"], lam_init, bt, st)
        o_b = _dec_mla(qlat, qpe, cache_ckv, cache_kpe, ckv32, kpe32, w["wuv_h"], bt, st)
    else:
        ckv32, kpe32, qcat, kcat, vmla = pl.pallas_call(
            _in_lat_prompt_kernel,
            grid=(n_tiles,),
            in_specs=lat_in + [_resident((KV_LORA, H_B * 128)), _resident((KV_LORA, H_B * 128))],
            out_specs=[tok(KV_LORA), tok(ROPE_DIM), tok(H_B * 256), tok(H_B * 256), tok(W_B)],
            out_shape=[tok_shape(KV_LORA, F32), tok_shape(ROPE_DIM, F32),
                       tok_shape(H_B * 256, BF16), tok_shape(H_B * 256, BF16), tok_shape(W_B, BF16)],
            compiler_params=_params("arbitrary"),
            name="in_lat_prompt",
        )(*lat_args, w["wuk"], w["wuv"])
        o_a = _flash_diff(qa.reshape(bt, st, W_A), k16.reshape(bt, st, W_A), v16.reshape(bt, st, W_A),
                          lam_refs, w["g_subln"], lam_init, bt, st).reshape(tokens, W_A)
        o_b = _flash_mla(qcat.reshape(bt, st, H_B * 256), kcat.reshape(bt, st, H_B * 256),
                         vmla.reshape(bt, st, W_B), bt, st).reshape(tokens, W_B)

    x1 = pl.pallas_call(
        _out_proj_kernel,
        grid=(n_tiles,),
        in_specs=[tok(W_A), tok(W_B), tok(W_ATT), x_spec, mod_spec(0, 2), _resident((W_ATT, D_MODEL))],
        out_specs=x_spec,
        out_shape=jax.ShapeDtypeStruct(x3.shape, F32),
        compiler_params=_params("arbitrary"),
        name="out_proj",
    )(o_a, o_b, sgate, x3, mod, w["wout_att"])

    if sample:
        p_nb, p_rows, p_tiles, p_tps = nb, st, n_tiles, 1
        px_spec = x_spec
        pmod = mod_spec
        pre_spec = pl.BlockSpec((nb, POOL_PREFIX, W_C), lambda i: (i, 0, 0))
        st_spec = pl.BlockSpec((nb, POOL_PREFIX, W_C), lambda i: (i, 0, 0))
        pos0 = cache[0].shape[1]
        prefix = w["pool_prefix"]
    else:
        p_nb, p_rows = 1, tm_pool
        p_tps = st // p_rows
        p_tiles = bt * p_tps
        px_spec = pl.BlockSpec((1, p_rows, D_MODEL), lambda i: (i // p_tps, i % p_tps, 0))

        def pmod(layer, comp):
            base = (layer * 3 + comp) * MOD_ROWS + 16
            return pl.BlockSpec((1, 1, D_MODEL), lambda i: (base + i // p_tps, 0, 0))

        pre_spec = pl.BlockSpec((1, POOL_PREFIX, W_C), lambda i: (0, 0, 0))
        st_spec = pl.BlockSpec((1, POOL_PREFIX, W_C), lambda i: (i // p_tps, 0, 0))
        pos0 = 0
        prefix = jnp.zeros((1, POOL_PREFIX, W_C), F32)

    y, state = pl.pallas_call(
        functools.partial(_pool_kernel, p_tps, pos0, not sample),
        grid=(p_tiles,),
        in_specs=[px_spec, pmod(1, 0), pmod(1, 1), pmod(1, 2), vec(D_MODEL), pre_spec,
                  _resident((D_MODEL, 2 * W_C)), _resident((N_POOL_GROUPS, POOL_GW, POOL_GW)),
                  vec(W_C), _resident((W_C, D_MODEL)), vec(D_MODEL)],
        out_specs=[px_spec, st_spec],
        out_shape=[jax.ShapeDtypeStruct(x3.shape, F32),
                   jax.ShapeDtypeStruct((bt, POOL_PREFIX, W_C), F32)],
        scratch_shapes=[pltpu.VMEM((p_nb, POOL_PREFIX + p_rows, W_C), F32)],
        compiler_params=_params("arbitrary"),
        name="pool",
    )(x1, mod, mod, mod, w["g_norm1"], prefix, w["win_pool"], w["wgrp"], w["scale_pool"],
      w["wout_pool"], w["g_final"])

    return (y,
            k32.reshape(1, bt, st, H_A, DV_A), v32.reshape(1, bt, st, H_A, DV_A),
            ckv32.reshape(1, bt, st, KV_LORA), kpe32.reshape(1, bt, st, ROPE_DIM),
            state[None, :, 1:, :])


def kernel(x_prompt, x_sample, c_prompt, c_sample, cache_diff_k, cache_diff_v, cache_mla_ckv, cache_mla_kpe, state_pool, g_norm, w_ada, b_ada, w_in_att, w_out_att, lam_q1, lam_k1, lam_q2, lam_k2, g_subln, g_qa, w_uq, w_uqr, w_uk, w_uv, g_kva, w_in_pool, w_grp_pool, scale_pool, w_out_pool, g_final):
    b_p, s_p, _ = x_prompt.shape
    b_s, n_s, _ = x_sample.shape
    past = cache_diff_k.shape[2]

    c_all = jnp.concatenate(
        [c_sample, c_prompt, jnp.zeros((MOD_ROWS - b_s - b_p, D_MODEL), F32)], axis=0)
    mod = _ada(c_all, w_ada, b_ada).reshape(-1, 1, D_MODEL)

    w_in = w_in_att[0]
    kpe_w = w_in[:, 4 * W_A:4 * W_A + ROPE_DIM]
    zpad = jnp.zeros((D_MODEL, 128 - ROPE_DIM), F32)
    uqr = w_uqr[0]
    pad_heads = lambda a: jnp.pad(a, ((0, 0), (0, 0), (0, 128 - ROPE_DIM))).reshape(Q_LORA, H_B * 128)
    w = {
        "g_norm0": g_norm[0:1], "g_norm1": g_norm[1:2],
        "wqkv": w_in[:, :3 * W_A].astype(BF16),
        "wc": w_in[:, 3 * W_A:4 * W_A].astype(BF16),
        "wkpe": jnp.concatenate([kpe_w, zpad, _rot_half_cols(kpe_w), zpad], axis=1).astype(BF16),
        "wgate": w_in[:, 4 * W_A + ROPE_DIM:].astype(BF16),
        "g_qa": g_qa, "g_kva": g_kva,
        "wuq": w_uq[0].reshape(Q_LORA, H_B * NOPE_DIM).astype(BF16),
        "wuqr": pad_heads(uqr).astype(BF16),
        "wuqrr": pad_heads(_rot_half_cols(uqr)).astype(BF16),
        "wuk": w_uk[0].reshape(KV_LORA, H_B * NOPE_DIM).astype(BF16),
        "wuv": w_uv[0].reshape(KV_LORA, H_B * V_DIM).astype(BF16),
        "wukt": jnp.transpose(w_uk[0], (1, 2, 0)).astype(BF16),
        "wuv_h": jnp.transpose(w_uv[0], (1, 0, 2)).astype(BF16),
        "wout_att": w_out_att[0].astype(BF16),
        "lam_q1": lam_q1, "lam_k1": lam_k1, "lam_q2": lam_q2, "lam_k2": lam_k2,
        "g_subln": g_subln,
        "win_pool": w_in_pool[0].astype(BF16),
        "wgrp": w_grp_pool[0].astype(BF16),
        "scale_pool": scale_pool,
        "wout_pool": w_out_pool[0].astype(BF16),
        "g_final": g_final.reshape(1, D_MODEL),
        "pool_prefix": jnp.pad(state_pool[0], ((0, 0), (1, 0), (0, 0))),
    }

    cos_p, sin_p = _rope_tables(jnp.arange(s_p, dtype=jnp.int32))
    cos_s, sin_s = _rope_tables(past + jnp.arange(n_s, dtype=jnp.int32))
    cos_s = jnp.tile(cos_s, (SAMPLE_TILE_SEQS, 1))
    sin_s = jnp.tile(sin_s, (SAMPLE_TILE_SEQS, 1))

    out_p = _group(x_prompt, mod, cos_p, sin_p, False, None, w, 256, 256)
    cache = (cache_diff_k[0].reshape(b_s, past, W_A), cache_diff_v[0].reshape(b_s, past, W_A),
             cache_mla_ckv[0], cache_mla_kpe[0])
    out_s = _group(x_sample, mod, cos_s, sin_s, True, cache, w, 0, 0)
    return (out_p[0], out_s[0]) + out_p[1:] + out_s[1:]
```

```python
import functools
import math

import jax
import jax.numpy as jnp
from jax import lax
from jax.experimental import pallas as pl
from jax.experimental.pallas import tpu as pltpu

F32 = jnp.float32
BF16 = jnp.bfloat16

D_MODEL = 2048
CHUNK = 64
CHUNK_SHIFT = 6
H_A = 8
DQK = 64
DQK_SHIFT = 6
DV_A = 2 * DQK
W_A = H_A * DV_A
H_B = 8
Q_LORA = 512
KV_LORA = 512
NOPE_DIM = 128
ROPE_DIM = 64
V_DIM = 128
W_B = H_B * V_DIM
ROPE_THETA = 10000.0
W_ATT = W_A + W_B
POOL_WINDOWS = (2, 4, 8, 16)
N_POOL_GROUPS = 4
W_C = D_MODEL
POOL_GW = W_C // N_POOL_GROUPS
POOL_STATE = max(POOL_WINDOWS) - 1
POOL_PREFIX = POOL_STATE + 1
EPS = 1e-6
SUBLN_EPS = 1e-5
NEG_INF = -1e30
LOG2E = math.log2(math.e)
DIFF_QSCALE = DQK ** -0.5 * LOG2E
MLA_QSCALE = (NOPE_DIM + ROPE_DIM) ** -0.5 * LOG2E
MOD_ROWS = 32
FLASH_DIFF_HEADS = 1
FLASH_MLA_HEADS = 2
SAMPLE_TILE_SEQS = 8
VMEM_LIMIT = 56 * 1024 * 1024


def _lambda_init(layer_idx):
    return 0.8 - 0.6 * math.exp(-0.3 * layer_idx)


def _rms(x, eps):
    return x * lax.rsqrt(jnp.mean(x * x, axis=-1, keepdims=True) + eps)


def _silu(x):
    return x * (1.0 / (1.0 + jnp.exp(-x)))


def _dot(a, b):
    return jnp.dot(a, b, preferred_element_type=F32)


def _dot_nt(a, b):
    return lax.dot_general(a, b, (((1,), (1,)), ((), ())), preferred_element_type=F32)


def _resident(shape):
    nd = len(shape)
    return pl.BlockSpec(shape, lambda *_: (0,) * nd, pipeline_mode=pl.Buffered(1))


def _params(*sem):
    return pltpu.CompilerParams(dimension_semantics=sem, vmem_limit_bytes=VMEM_LIMIT)


def _ada_kernel(c_ref, w_ref, b_ref, o_ref):
    cond = _silu(c_ref[...]).astype(BF16)
    w = w_ref[0].astype(BF16)
    o_ref[0, 0] = _dot(cond, w) + b_ref[0, 0]


def _ada(c_all, w_ada, b_ada):
    depth = w_ada.shape[0]
    tn = 1024
    per = D_MODEL // tn
    return pl.pallas_call(
        _ada_kernel,
        grid=(depth, 3 * per),
        in_specs=[
            pl.BlockSpec((MOD_ROWS, D_MODEL), lambda l, j: (0, 0)),
            pl.BlockSpec((1, D_MODEL, tn), lambda l, j: (l, 0, j)),
            pl.BlockSpec((1, 1, 1, tn), lambda l, j: (l, j // per, 0, j % per)),
        ],
        out_specs=pl.BlockSpec((1, 1, MOD_ROWS, tn), lambda l, j: (l, j // per, 0, j % per)),
        out_shape=jax.ShapeDtypeStruct((depth, 3, MOD_ROWS, D_MODEL), F32),
        compiler_params=_params("arbitrary", "arbitrary"),
        name="ada",
    )(c_all, w_ada, b_ada.reshape(depth, 3, 1, D_MODEL))


def _prenorm(x, gn, scale, shift):
    return _rms(x, EPS) * gn * (1.0 + scale) + shift


def _in_qkv_kernel(x_ref, shift_ref, scale_ref, gn_ref, wqkv_ref, wgate_ref,
                   qa_ref, k32_ref, v32_ref, sg_ref, *flash_refs):
    nb, rows, _ = x_ref.shape
    h = _prenorm(x_ref[...], gn_ref[...], scale_ref[...], shift_ref[...])
    hb = h.reshape(nb * rows, D_MODEL).astype(BF16)
    z = _dot(hb, wqkv_ref[...])
    qa_ref[...] = (z[:, :W_A] * DIFF_QSCALE).astype(BF16)
    ka = z[:, W_A:2 * W_A]
    va = z[:, 2 * W_A:]
    for hd in range(H_A):
        k32_ref[:, hd, :] = ka[:, hd * DV_A:(hd + 1) * DV_A]
        v32_ref[:, hd, :] = va[:, hd * DV_A:(hd + 1) * DV_A]
    if flash_refs:
        k16_ref, vt_ref = flash_refs
        k16_ref[...] = ka.astype(BF16)
        vt_ref[0, 0] = va.T.astype(BF16)
    sg_ref[...] = _silu(_dot(hb, wgate_ref[...])).astype(BF16)


def _latent_common(x_ref, shift_ref, scale_ref, gn_ref, cos_ref, sin_ref, wc_ref, wkpe_ref,
                   gqa_ref, gkva_ref, wuq_ref, wuqr_ref, wuqrr_ref, ckv32_ref, kpe32_ref):
    nb, rows, _ = x_ref.shape
    h = _prenorm(x_ref[...], gn_ref[...], scale_ref[...], shift_ref[...])
    hb = h.reshape(nb * rows, D_MODEL).astype(BF16)
    zc = _dot(hb, wc_ref[...])
    cq = (_rms(zc[:, :Q_LORA], EPS) * gqa_ref[...]).astype(BF16)
    ckv = _rms(zc[:, Q_LORA:], EPS) * gkva_ref[...]
    ckv32_ref[...] = ckv
    cos = cos_ref[...]
    sin = sin_ref[...]
    zk = _dot(hb, wkpe_ref[...])
    kpe_pad = zk[:, :128] * cos + zk[:, 128:] * sin
    kpe32_ref[...] = kpe_pad[:, :ROPE_DIM]
    q_nope = _dot(cq, wuq_ref[...]) * MLA_QSCALE
    cos8 = jnp.tile(cos, (1, H_B))
    sin8 = jnp.tile(sin, (1, H_B))
    q_pe = (_dot(cq, wuqr_ref[...]) * cos8 + _dot(cq, wuqrr_ref[...]) * sin8) * MLA_QSCALE
    return ckv, kpe_pad, q_nope, q_pe


def _in_lat_prompt_kernel(x_ref, shift_ref, scale_ref, gn_ref, cos_ref, sin_ref, wc_ref, wkpe_ref,
                          gqa_ref, gkva_ref, wuq_ref, wuqr_ref, wuqrr_ref, wuk_ref, wuv_ref,
                          ckv32_ref, kpe32_ref, qcat_ref, kcat_ref, vmla_ref):
    ckv, kpe_pad, q_nope, q_pe = _latent_common(
        x_ref, shift_ref, scale_ref, gn_ref, cos_ref, sin_ref, wc_ref, wkpe_ref,
        gqa_ref, gkva_ref, wuq_ref, wuqr_ref, wuqrr_ref, ckv32_ref, kpe32_ref)
    ckvb = ckv.astype(BF16)
    k_nope = _dot(ckvb, wuk_ref[...])
    vmla_ref[0, 0] = _dot(ckvb, wuv_ref[...]).T.astype(BF16)
    kpe_b = kpe_pad.astype(BF16)
    for hd in range(H_B):
        sl = slice(hd * 128, (hd + 1) * 128)
        qcat_ref[:, hd * 256:hd * 256 + 128] = q_nope[:, sl].astype(BF16)
        qcat_ref[:, hd * 256 + 128:(hd + 1) * 256] = q_pe[:, sl].astype(BF16)
        kcat_ref[:, hd * 256:hd * 256 + 128] = k_nope[:, sl].astype(BF16)
        kcat_ref[:, hd * 256 + 128:(hd + 1) * 256] = kpe_b


def _in_lat_sample_kernel(x_ref, shift_ref, scale_ref, gn_ref, cos_ref, sin_ref, wc_ref, wkpe_ref,
                          gqa_ref, gkva_ref, wuq_ref, wuqr_ref, wuqrr_ref, wukt_ref,
                          ckv32_ref, kpe32_ref, qlat_ref, qpe_ref):
    _, _, q_nope, q_pe = _latent_common(
        x_ref, shift_ref, scale_ref, gn_ref, cos_ref, sin_ref, wc_ref, wkpe_ref,
        gqa_ref, gkva_ref, wuq_ref, wuqr_ref, wuqrr_ref, ckv32_ref, kpe32_ref)
    qn = q_nope.astype(BF16)
    for hd in range(H_B):
        qlat_ref[:, hd * KV_LORA:(hd + 1) * KV_LORA] = _dot(
            qn[:, hd * 128:(hd + 1) * 128], wukt_ref[hd]).astype(BF16)
    qpe_ref[...] = q_pe.astype(BF16)


def _online_softmax_step(s, v, m, l, acc):
    m_new = jnp.maximum(m, jnp.max(s, axis=1, keepdims=True))
    alpha = jnp.exp2(m - m_new)
    p = jnp.exp2(s - m_new)
    l = alpha * l + jnp.sum(p, axis=1, keepdims=True)
    acc = alpha * acc + _dot(p.astype(BF16), v)
    return m_new, l, acc


def _flash_cols(qs, k_ref, vt_ref, s_ref, acc_ref, dk, dv, q_tile, tq, tk, pos_of_col):
    cols = qs[0].shape[0]
    assert tq == 2 * tk
    heads = range(len(qs))
    q_chunk = pos_of_col(lax.broadcasted_iota(jnp.int32, (tk, cols), 1)) >> CHUNK_SHIFT
    k_row = lax.broadcasted_iota(jnp.int32, (tk, cols), 0)

    def diag_mask(d):
        return ((k_row + d * tk) >> CHUNK_SHIFT) <= q_chunk

    def scores(tile, slot, mask):
        off = pl.multiple_of(tile * tk, tk)
        col_max = []
        for g in heads:
            s = _dot_nt(k_ref[0, pl.ds(off, tk), g * dk:(g + 1) * dk], qs[g])
            if mask is not None:
                s = jnp.where(mask, s, NEG_INF)
            s_ref[slot, g] = s
            col_max.append(jnp.max(s, axis=0, keepdims=True))
        return tuple(col_max)

    def consume(tile, slot, col_max, ml):
        out = []
        for g in heads:
            m, l = ml[g]
            m_new = jnp.maximum(m, col_max[g])
            alpha = jnp.exp2(m - m_new)
            p = jnp.exp2(s_ref[slot, g] - m_new)
            l = alpha * l + jnp.sum(p, axis=0, keepdims=True)
            acc_ref[g] = alpha * acc_ref[g] + _dot(vt_ref[0, tile, g * dv:(g + 1) * dv, :], p.astype(BF16))
            out.append((m_new, l))
        return tuple(out)

    def pair(i, carry, next_mask):
        cm0, ml = carry
        cm1 = scores(2 * i + 1, 1, None)
        ml = consume(2 * i, 0, cm0, ml)
        cm0 = scores(2 * i + 2, 0, next_mask)
        ml = consume(2 * i + 1, 1, cm1, ml)
        return cm0, ml

    acc_ref[...] = jnp.zeros(acc_ref.shape, F32)
    ml = ((jnp.full((1, cols), NEG_INF, F32), jnp.zeros((1, cols), F32)),) * len(qs)
    cm0 = scores(0, 0, jnp.logical_or(q_tile > 0, diag_mask(0)))
    carry = lax.fori_loop(0, jnp.maximum(q_tile - 1, 0), lambda i, c: pair(i, c, None), (cm0, ml))
    carry = lax.cond(q_tile > 0, lambda c: pair(q_tile - 1, c, diag_mask(0)), lambda c: c, carry)
    cm0, ml = carry
    cm1 = scores(2 * q_tile + 1, 1, diag_mask(1))
    ml = consume(2 * q_tile, 0, cm0, ml)
    ml = consume(2 * q_tile + 1, 1, cm1, ml)
    return [(acc_ref[g], ml[g][1]) for g in heads]


def _lambda_full(lq1_ref, lk1_ref, lq2_ref, lk2_ref, lam_init):
    a = jnp.sum(lq1_ref[...] * lk1_ref[...], axis=-1, keepdims=True)
    b = jnp.sum(lq2_ref[...] * lk2_ref[...], axis=-1, keepdims=True)
    return jnp.exp(a) - jnp.exp(b) + lam_init


def _flash_diff_kernel(tq, tk, lam_init, q_ref, k_ref, vt_ref, lq1_ref, lk1_ref, lq2_ref, lk2_ref,
                       gsub_ref, o_ref, s_ref, acc_ref):
    heads = q_ref.shape[2] // DV_A
    qs = []
    for g in range(heads):
        q = q_ref[0, :, g * DV_A:(g + 1) * DV_A]
        lane = lax.broadcasted_iota(jnp.int32, q.shape, 1)
        zero = jnp.zeros_like(q)
        qs.append(jnp.concatenate(
            [jnp.where(lane < DQK, q, zero), jnp.where(lane >= DQK, q, zero)], axis=0))
    res = _flash_cols(qs, k_ref, vt_ref, s_ref, acc_ref, DV_A, DV_A, pl.program_id(2), tq, tk,
                      lambda r: jnp.where(r >= tq, r - tq, r))
    lam = _lambda_full(lq1_ref, lk1_ref, lq2_ref, lk2_ref, lam_init)
    for g, (acc, l) in enumerate(res):
        o = acc / l
        o = o[:, :tq] - lam * o[:, tq:]
        o = o * lax.rsqrt(jnp.mean(o * o, axis=0, keepdims=True) + SUBLN_EPS)
        o_ref[0, :, g * DV_A:(g + 1) * DV_A] = (
            o * gsub_ref[...] * (1.0 - lam_init)).T.astype(o_ref.dtype)


def _flash_mla_kernel(tq, tk, q_ref, k_ref, vt_ref, o_ref, s_ref, acc_ref):
    heads = q_ref.shape[2] // 256
    qs = [q_ref[0, :, g * 256:(g + 1) * 256] for g in range(heads)]
    res = _flash_cols(qs, k_ref, vt_ref, s_ref, acc_ref, 256, V_DIM, pl.program_id(2), tq, tk,
                      lambda r: r)
    for g, (acc, l) in enumerate(res):
        o_ref[0, :, g * V_DIM:(g + 1) * V_DIM] = (acc / l).T.astype(o_ref.dtype)


def _flash_diff(qa, k16, v16t, lam_refs, gsub_col, lam_init, b, s, tk):
    tq = 2 * tk
    heads = FLASH_DIFF_HEADS
    gw = heads * DV_A
    vec = pl.BlockSpec((1, DQK), lambda bi, hi, qi: (0, 0))
    return pl.pallas_call(
        functools.partial(_flash_diff_kernel, tq, tk, lam_init),
        grid=(b, H_A // heads, s // tq),
        in_specs=[
            pl.BlockSpec((1, tq, gw), lambda bi, hi, qi: (bi, qi, hi)),
            pl.BlockSpec((1, s, gw), lambda bi, hi, qi: (bi, 0, hi)),
            pl.BlockSpec((1, s // tk, gw, tk), lambda bi, hi, qi: (bi, 0, hi, 0)),
            vec, vec, vec, vec,
            pl.BlockSpec((DV_A, 1), lambda bi, hi, qi: (0, 0)),
        ],
        out_specs=pl.BlockSpec((1, tq, gw), lambda bi, hi, qi: (bi, qi, hi)),
        out_shape=jax.ShapeDtypeStruct((b, s, W_A), BF16),
        scratch_shapes=[pltpu.VMEM((2, heads, tk, 2 * tq), F32), pltpu.VMEM((heads, DV_A, 2 * tq), F32)],
        compiler_params=_params("arbitrary", "arbitrary", "arbitrary"),
        name="flash_diff",
    )(qa, k16, v16t, *lam_refs, gsub_col)


def _flash_mla(qcat, kcat, vmlat, b, s, tk):
    tq = 2 * tk
    g = FLASH_MLA_HEADS
    return pl.pallas_call(
        functools.partial(_flash_mla_kernel, tq, tk),
        grid=(b, H_B // g, s // tq),
        in_specs=[
            pl.BlockSpec((1, tq, g * 256), lambda bi, hi, qi: (bi, qi, hi)),
            pl.BlockSpec((1, s, g * 256), lambda bi, hi, qi: (bi, 0, hi)),
            pl.BlockSpec((1, s // tk, g * V_DIM, tk), lambda bi, hi, qi: (bi, 0, hi, 0)),
        ],
        out_specs=pl.BlockSpec((1, tq, g * V_DIM), lambda bi, hi, qi: (bi, qi, hi)),
        out_shape=jax.ShapeDtypeStruct((b, s, W_B), BF16),
        scratch_shapes=[pltpu.VMEM((2, g, tk, tq), F32), pltpu.VMEM((g, V_DIM, tq), F32)],
        compiler_params=_params("arbitrary", "arbitrary", "arbitrary"),
        name="flash_mla",
    )(qcat, kcat, vmlat)


def _dec_update(s, v, m_ref, l_ref, acc_ref, g):
    m_new, l, acc = _online_softmax_step(s, v, m_ref[g], l_ref[g], acc_ref[g])
    m_ref[g] = m_new
    l_ref[g] = l
    acc_ref[g] = acc


def _dec_diff_kernel(n, lam_init, q_ref, k_ref, v_ref, kn_ref, vn_ref, lq1_ref, lk1_ref, lq2_ref,
                     lk2_ref, gsub_ref, o_ref, qbd_ref, m_ref, l_ref, acc_ref):
    j = pl.program_id(1)
    gw = 4 * DV_A

    @pl.when(j == 0)
    def _():
        q = q_ref[...]
        for g in range(2):
            qg = q[:, g * gw:(g + 1) * gw]
            lane = lax.broadcasted_iota(jnp.int32, qg.shape, 1)
            blocks = [jnp.where((lane >> DQK_SHIFT) == blk, qg, jnp.zeros_like(qg))
                      for blk in range(8)]
            qbd_ref[g] = jnp.concatenate(blocks, axis=0)
        m_ref[...] = jnp.full(m_ref.shape, NEG_INF, F32)
        l_ref[...] = jnp.zeros(l_ref.shape, F32)
        acc_ref[...] = jnp.zeros(acc_ref.shape, F32)

    def group_lanes(ref, g):
        return jnp.concatenate([ref[:, g * 4 + hl, :] for hl in range(4)], axis=1).astype(BF16)

    for g in range(2):
        _dec_update(_dot_nt(qbd_ref[g], group_lanes(k_ref.at[0], g)), group_lanes(v_ref.at[0], g),
                    m_ref, l_ref, acc_ref, g)

    @pl.when(j == pl.num_programs(1) - 1)
    def _():
        lam = _lambda_full(lq1_ref, lk1_ref, lq2_ref, lk2_ref, lam_init)
        for g in range(2):
            _dec_update(_dot_nt(qbd_ref[g], group_lanes(kn_ref, g)), group_lanes(vn_ref, g),
                        m_ref, l_ref, acc_ref, g)
            o = acc_ref[g] / l_ref[g]
            for hl in range(4):
                cols = slice(hl * DV_A, (hl + 1) * DV_A)
                o0 = o[(2 * hl) * n:(2 * hl + 1) * n, cols]
                o1 = o[(2 * hl + 1) * n:(2 * hl + 2) * n, cols]
                od = _rms(o0 - lam * o1, SUBLN_EPS) * gsub_ref[...] * (1.0 - lam_init)
                hd = g * 4 + hl
                o_ref[:, hd * DV_A:(hd + 1) * DV_A] = od.astype(o_ref.dtype)


def _dec_diff(qa, cache_k, cache_v, k_new, v_new, lam_refs, gsub, lam_init, b, n):
    past = cache_k.shape[1]
    tk = 1024
    vec = pl.BlockSpec((1, DQK), lambda bi, j: (0, 0))
    rows = pl.BlockSpec((n, W_A), lambda bi, j: (bi, 0))
    new_rows = pl.BlockSpec((n, H_A, DV_A), lambda bi, j: (bi, 0, 0))
    cache_tile = pl.BlockSpec((1, tk, H_A, DV_A), lambda bi, j: (bi, j, 0, 0))
    return pl.pallas_call(
        functools.partial(_dec_diff_kernel, n, lam_init),
        grid=(b, past // tk),
        in_specs=[
            rows, cache_tile, cache_tile, new_rows, new_rows, vec, vec, vec, vec,
            pl.BlockSpec((1, DV_A), lambda bi, j: (0, 0)),
        ],
        out_specs=rows,
        out_shape=jax.ShapeDtypeStruct((b * n, W_A), BF16),
        scratch_shapes=[
            pltpu.VMEM((2, 8 * n, 4 * DV_A), BF16),
            pltpu.VMEM((2, 8 * n, 1), F32),
            pltpu.VMEM((2, 8 * n, 1), F32),
            pltpu.VMEM((2, 8 * n, 4 * DV_A), F32),
        ],
        compiler_params=_params("arbitrary", "arbitrary"),
        name="dec_diff",
    )(qa, cache_k, cache_v, k_new, v_new, *lam_refs, gsub)


def _dec_mla_kernel(n, qlat_ref, qpe_ref, ckv_ref, kpe_ref, ckvn_ref, kpen_ref, wuv_ref, o_ref,
                    ql_ref, qp_ref, m_ref, l_ref, acc_ref):
    j = pl.program_id(1)

    @pl.when(j == 0)
    def _():
        for hd in range(H_B):
            ql_ref[hd * n:(hd + 1) * n, :] = qlat_ref[:, hd * KV_LORA:(hd + 1) * KV_LORA]
            qp_ref[hd * n:(hd + 1) * n, :] = qpe_ref[:, hd * 128:(hd + 1) * 128]
        m_ref[...] = jnp.full(m_ref.shape, NEG_INF, F32)
        l_ref[...] = jnp.zeros(l_ref.shape, F32)
        acc_ref[...] = jnp.zeros(acc_ref.shape, F32)

    def update(ckv, kpe):
        ckv = ckv.astype(BF16)
        s = _dot_nt(ql_ref[...], ckv) + _dot_nt(qp_ref[:, :ROPE_DIM], kpe.astype(BF16))
        _dec_update(s, ckv, m_ref, l_ref, acc_ref, 0)

    update(ckv_ref[0], kpe_ref[0])

    @pl.when(j == pl.num_programs(1) - 1)
    def _():
        update(ckvn_ref[...], kpen_ref[...])
        o = (acc_ref[0] / l_ref[0]).astype(BF16)
        for hd in range(H_B):
            o_ref[:, hd * V_DIM:(hd + 1) * V_DIM] = _dot(
                o[hd * n:(hd + 1) * n], wuv_ref[hd]).astype(o_ref.dtype)


def _dec_mla(qlat, qpe, cache_ckv, cache_kpe, ckv_new, kpe_new, wuv_h, b, n):
    past = cache_ckv.shape[1]
    tk = 1024
    return pl.pallas_call(
        functools.partial(_dec_mla_kernel, n),
        grid=(b, past // tk),
        in_specs=[
            pl.BlockSpec((n, H_B * KV_LORA), lambda bi, j: (bi, 0)),
            pl.BlockSpec((n, H_B * 128), lambda bi, j: (bi, 0)),
            pl.BlockSpec((1, tk, KV_LORA), lambda bi, j: (bi, j, 0)),
            pl.BlockSpec((1, tk, ROPE_DIM), lambda bi, j: (bi, j, 0)),
            pl.BlockSpec((n, KV_LORA), lambda bi, j: (bi, 0)),
            pl.BlockSpec((n, ROPE_DIM), lambda bi, j: (bi, 0)),
            pl.BlockSpec((H_B, KV_LORA, V_DIM), lambda bi, j: (0, 0, 0)),
        ],
        out_specs=pl.BlockSpec((n, W_B), lambda bi, j: (bi, 0)),
        out_shape=jax.ShapeDtypeStruct((b * n, W_B), BF16),
        scratch_shapes=[
            pltpu.VMEM((H_B * n, KV_LORA), BF16),
            pltpu.VMEM((H_B * n, 128), BF16),
            pltpu.VMEM((1, H_B * n, 1), F32),
            pltpu.VMEM((1, H_B * n, 1), F32),
            pltpu.VMEM((1, H_B * n, KV_LORA), F32),
        ],
        compiler_params=_params("arbitrary", "arbitrary"),
        name="dec_mla",
    )(qlat, qpe, cache_ckv, cache_kpe, ckv_new, kpe_new, wuv_h)


def _out_proj_kernel(oa_ref, ob_ref, sg_ref, x_ref, gate_ref, w_ref, o_ref):
    nb, rows, _ = x_ref.shape
    sg = sg_ref[...].astype(F32)
    oa = (oa_ref[...].astype(F32) * sg[:, :W_A]).astype(BF16)
    ob = (ob_ref[...].astype(F32) * sg[:, W_A:]).astype(BF16)
    y = _dot(oa, w_ref[:W_A, :]) + _dot(ob, w_ref[W_A:, :])
    o_ref[...] = x_ref[...] + gate_ref[...] * y.reshape(nb, rows, D_MODEL)


def _pool_kernel(tiles_per_seq, pos0, carry, x_ref, shift_ref, scale_ref, gate_ref, gn_ref, pre_ref,
                 win_ref, wgrp_ref, sp_ref, wout_ref, gf_ref, y_ref, st_ref, uext_ref):
    nb, rows, _ = x_ref.shape
    tm = nb * rows
    t = pl.program_id(0) % tiles_per_seq
    x = x_ref[...]
    h = _prenorm(x, gn_ref[...], scale_ref[...], shift_ref[...])
    z = _dot(h.reshape(tm, D_MODEL).astype(BF16), win_ref[...])
    u = z[:, :W_C].reshape(nb, rows, W_C)
    sgate = _silu(z[:, W_C:])

    if carry:
        @pl.when(t == 0)
        def _():
            uext_ref[:, :POOL_PREFIX, :] = pre_ref[...]
    else:
        uext_ref[:, :POOL_PREFIX, :] = pre_ref[...]
    uext_ref[:, POOL_PREFIX:, :] = u

    pos = pos0 + t * rows + lax.broadcasted_iota(jnp.int32, (1, rows, 1), 1)
    parts = []
    for g, w in enumerate(POOL_WINDOWS):
        cols = slice(g * POOL_GW, (g + 1) * POOL_GW)
        ws = u[:, :, cols]
        for k in range(1, w):
            ws = ws + uext_ref[:, POOL_PREFIX - k:POOL_PREFIX - k + rows, cols]
        cnt = jnp.minimum(pos + 1, w).astype(F32)
        dgrp = (ws / cnt - u[:, :, cols]).reshape(tm, POOL_GW).astype(BF16)
        parts.append(_dot(dgrp, wgrp_ref[g]))
    m = jnp.concatenate(parts, axis=1) * sp_ref[...]
    y = _dot((m * sgate).astype(BF16), wout_ref[...])
    x2 = x + gate_ref[...] * y.reshape(nb, rows, D_MODEL)
    y_ref[...] = _rms(x2, EPS) * gf_ref[...]
    tail = uext_ref[:, rows:rows + POOL_PREFIX, :]
    st_ref[...] = tail
    if carry:
        uext_ref[:, :POOL_PREFIX, :] = tail


def _rot_half_cols(w):
    half = ROPE_DIM // 2
    return jnp.concatenate([-w[..., half:], w[..., :half]], axis=-1)


def _rope_tables(pos):
    half = ROPE_DIM // 2
    inv = ROPE_THETA ** (-jnp.arange(half, dtype=F32) / half)
    ang = pos.astype(F32)[:, None] * inv[None, :]
    cos, sin = jnp.cos(ang), jnp.sin(ang)
    zero = jnp.zeros((pos.shape[0], 128 - ROPE_DIM), F32)
    return jnp.concatenate([cos, cos, zero], axis=1), jnp.concatenate([sin, sin, zero], axis=1)


def _group(x3, mod, cos, sin, sample, cache, w, tm_att, tm_pool):
    bt, st, _ = x3.shape
    tokens = bt * st
    lam_init = _lambda_init(0)

    if sample:
        nb, rows, tps = SAMPLE_TILE_SEQS, st, 1
        n_tiles = bt // nb
        x_spec = pl.BlockSpec((nb, rows, D_MODEL), lambda i: (i, 0, 0))

        def mod_spec(layer, comp):
            blk = (layer * 3 + comp) * MOD_ROWS // nb
            return pl.BlockSpec((nb, 1, D_MODEL), lambda i: (blk + i, 0, 0))
    else:
        nb, rows = 1, tm_att
        tps = st // rows
        n_tiles = bt * tps
        x_spec = pl.BlockSpec((1, rows, D_MODEL), lambda i: (i // tps, i % tps, 0))

        def mod_spec(layer, comp):
            base = (layer * 3 + comp) * MOD_ROWS + 16
            return pl.BlockSpec((1, 1, D_MODEL), lambda i: (base + i // tps, 0, 0))

    tm = nb * rows

    def tok(width):
        return pl.BlockSpec((tm, width), lambda i: (i, 0))

    def tok_shape(width, dt):
        return jax.ShapeDtypeStruct((tokens, width), dt)

    vec = lambda width: pl.BlockSpec((1, width), lambda i: (0, 0))
    if sample:
        tab = pl.BlockSpec((tm, 128), lambda i: (0, 0))
    else:
        tab = pl.BlockSpec((tm, 128), lambda i: (i % tps, 0))

    heads_spec = pl.BlockSpec((tm, H_A, DV_A), lambda i: (i, 0, 0))
    heads_shape = jax.ShapeDtypeStruct((tokens, H_A, DV_A), F32)
    vt_spec = pl.BlockSpec((1, 1, W_A, tm), lambda i: (i // tps, i % tps, 0, 0))
    vt_shape = jax.ShapeDtypeStruct((bt, st // tm, W_A, tm), BF16)
    qkv_out = pl.pallas_call(
        _in_qkv_kernel,
        grid=(n_tiles,),
        in_specs=[x_spec, mod_spec(0, 0), mod_spec(0, 1), vec(D_MODEL),
                  _resident((D_MODEL, 3 * W_A)), _resident((D_MODEL, W_ATT))],
        out_specs=[tok(W_A), heads_spec, heads_spec, tok(W_ATT)] + ([] if sample else [tok(W_A), vt_spec]),
        out_shape=[tok_shape(W_A, BF16), heads_shape, heads_shape, tok_shape(W_ATT, BF16)]
        + ([] if sample else [tok_shape(W_A, BF16), vt_shape]),
        compiler_params=_params("arbitrary"),
        name="in_qkv",
    )(x3, mod, mod, w["g_norm0"], w["wqkv"], w["wgate"])
    qa, k32, v32, sgate = qkv_out[:4]

    lat_in = [x_spec, mod_spec(0, 0), mod_spec(0, 1), vec(D_MODEL), tab, tab,
              _resident((D_MODEL, Q_LORA + KV_LORA)), _resident((D_MODEL, 256)),
              vec(Q_LORA), vec(KV_LORA),
              _resident((Q_LORA, H_B * 128)), _resident((Q_LORA, H_B * 128)), _resident((Q_LORA, H_B * 128))]
    lat_args = [x3, mod, mod, w["g_norm0"], cos, sin, w["wc"], w["wkpe"], w["g_qa"], w["g_kva"],
                w["wuq"], w["wuqr"], w["wuqrr"]]
    lam_refs = (w["lam_q1"], w["lam_k1"], w["lam_q2"], w["lam_k2"])

    if sample:
        ckv32, kpe32, qlat, qpe = pl.pallas_call(
            _in_lat_sample_kernel,
            grid=(n_tiles,),
            in_specs=lat_in + [_resident((H_B, 128, KV_LORA))],
            out_specs=[tok(KV_LORA), tok(ROPE_DIM), tok(H_B * KV_LORA), tok(H_B * 128)],
            out_shape=[tok_shape(KV_LORA, F32), tok_shape(ROPE_DIM, F32),
                       tok_shape(H_B * KV_LORA, BF16), tok_shape(H_B * 128, BF16)],
            compiler_params=_params("arbitrary"),
            name="in_lat_sample",
        )(*lat_args, w["wukt"])
        cache_k, cache_v, cache_ckv, cache_kpe = cache
        o_a = _dec_diff(qa, cache_k, cache_v, k32, v32, lam_refs, w["g_subln"], lam_init, bt, st)
        o_b = _dec_mla(qlat, qpe, cache_ckv, cache_kpe, ckv32, kpe32, w["wuv_h"], bt, st)
    else:
        ckv32, kpe32, qcat, kcat, vmla = pl.pallas_call(
            _in_lat_prompt_kernel,
            grid=(n_tiles,),
            in_specs=lat_in + [_resident((KV_LORA, H_B * 128)), _resident((KV_LORA, H_B * 128))],
            out_specs=[tok(KV_LORA), tok(ROPE_DIM), tok(H_B * 256), tok(H_B * 256), vt_spec],
            out_shape=[tok_shape(KV_LORA, F32), tok_shape(ROPE_DIM, F32),
                       tok_shape(H_B * 256, BF16), tok_shape(H_B * 256, BF16), vt_shape],
            compiler_params=_params("arbitrary"),
            name="in_lat_prompt",
        )(*lat_args, w["wuk"], w["wuv"])
        k16, v16t = qkv_out[4:]
        o_a = _flash_diff(qa.reshape(bt, st, W_A), k16.reshape(bt, st, W_A), v16t, lam_refs,
                          w["g_subln"].reshape(DV_A, 1), lam_init, bt, st, tm).reshape(tokens, W_A)
        o_b = _flash_mla(qcat.reshape(bt, st, H_B * 256), kcat.reshape(bt, st, H_B * 256),
                         vmla, bt, st, tm).reshape(tokens, W_B)

    x1 = pl.pallas_call(
        _out_proj_kernel,
        grid=(n_tiles,),
        in_specs=[tok(W_A), tok(W_B), tok(W_ATT), x_spec, mod_spec(0, 2), _resident((W_ATT, D_MODEL))],
        out_specs=x_spec,
        out_shape=jax.ShapeDtypeStruct(x3.shape, F32),
        compiler_params=_params("arbitrary"),
        name="out_proj",
    )(o_a, o_b, sgate, x3, mod, w["wout_att"])

    if sample:
        p_nb, p_rows, p_tiles, p_tps = nb, st, n_tiles, 1
        px_spec = x_spec
        pmod = mod_spec
        pre_spec = pl.BlockSpec((nb, POOL_PREFIX, W_C), lambda i: (i, 0, 0))
        st_spec = pl.BlockSpec((nb, POOL_PREFIX, W_C), lambda i: (i, 0, 0))
        pos0 = cache[0].shape[1]
        prefix = w["pool_prefix"]
    else:
        p_nb, p_rows = 1, tm_pool
        p_tps = st // p_rows
        p_tiles = bt * p_tps
        px_spec = pl.BlockSpec((1, p_rows, D_MODEL), lambda i: (i // p_tps, i % p_tps, 0))

        def pmod(layer, comp):
            base = (layer * 3 + comp) * MOD_ROWS + 16
            return pl.BlockSpec((1, 1, D_MODEL), lambda i: (base + i // p_tps, 0, 0))

        pre_spec = pl.BlockSpec((1, POOL_PREFIX, W_C), lambda i: (0, 0, 0))
        st_spec = pl.BlockSpec((1, POOL_PREFIX, W_C), lambda i: (i // p_tps, 0, 0))
        pos0 = 0
        prefix = jnp.zeros((1, POOL_PREFIX, W_C), F32)

    y, state = pl.pallas_call(
        functools.partial(_pool_kernel, p_tps, pos0, not sample),
        grid=(p_tiles,),
        in_specs=[px_spec, pmod(1, 0), pmod(1, 1), pmod(1, 2), vec(D_MODEL), pre_spec,
                  _resident((D_MODEL, 2 * W_C)), _resident((N_POOL_GROUPS, POOL_GW, POOL_GW)),
                  vec(W_C), _resident((W_C, D_MODEL)), vec(D_MODEL)],
        out_specs=[px_spec, st_spec],
        out_shape=[jax.ShapeDtypeStruct(x3.shape, F32),
                   jax.ShapeDtypeStruct((bt, POOL_PREFIX, W_C), F32)],
        scratch_shapes=[pltpu.VMEM((p_nb, POOL_PREFIX + p_rows, W_C), F32)],
        compiler_params=_params("arbitrary"),
        name="pool",
    )(x1, mod, mod, mod, w["g_norm1"], prefix, w["win_pool"], w["wgrp"], w["scale_pool"],
      w["wout_pool"], w["g_final"])

    return (y,
            k32.reshape(1, bt, st, H_A, DV_A), v32.reshape(1, bt, st, H_A, DV_A),
            ckv32.reshape(1, bt, st, KV_LORA), kpe32.reshape(1, bt, st, ROPE_DIM),
            state[None, :, 1:, :])


def kernel(x_prompt, x_sample, c_prompt, c_sample, cache_diff_k, cache_diff_v, cache_mla_ckv, cache_mla_kpe, state_pool, g_norm, w_ada, b_ada, w_in_att, w_out_att, lam_q1, lam_k1, lam_q2, lam_k2, g_subln, g_qa, w_uq, w_uqr, w_uk, w_uv, g_kva, w_in_pool, w_grp_pool, scale_pool, w_out_pool, g_final):
    b_p, s_p, _ = x_prompt.shape
    b_s, n_s, _ = x_sample.shape
    past = cache_diff_k.shape[2]

    c_all = jnp.concatenate(
        [c_sample, c_prompt, jnp.zeros((MOD_ROWS - b_s - b_p, D_MODEL), F32)], axis=0)
    mod = _ada(c_all, w_ada, b_ada).reshape(-1, 1, D_MODEL)

    w_in = w_in_att[0]
    kpe_w = w_in[:, 4 * W_A:4 * W_A + ROPE_DIM]
    zpad = jnp.zeros((D_MODEL, 128 - ROPE_DIM), F32)
    uqr = w_uqr[0]
    pad_heads = lambda a: jnp.pad(a, ((0, 0), (0, 0), (0, 128 - ROPE_DIM))).reshape(Q_LORA, H_B * 128)
    w = {
        "g_norm0": g_norm[0:1], "g_norm1": g_norm[1:2],
        "wqkv": w_in[:, :3 * W_A].astype(BF16),
        "wc": w_in[:, 3 * W_A:4 * W_A].astype(BF16),
        "wkpe": jnp.concatenate([kpe_w, zpad, _rot_half_cols(kpe_w), zpad], axis=1).astype(BF16),
        "wgate": w_in[:, 4 * W_A + ROPE_DIM:].astype(BF16),
        "g_qa": g_qa, "g_kva": g_kva,
        "wuq": w_uq[0].reshape(Q_LORA, H_B * NOPE_DIM).astype(BF16),
        "wuqr": pad_heads(uqr).astype(BF16),
        "wuqrr": pad_heads(_rot_half_cols(uqr)).astype(BF16),
        "wuk": w_uk[0].reshape(KV_LORA, H_B * NOPE_DIM).astype(BF16),
        "wuv": w_uv[0].reshape(KV_LORA, H_B * V_DIM).astype(BF16),
        "wukt": jnp.transpose(w_uk[0], (1, 2, 0)).astype(BF16),
        "wuv_h": jnp.transpose(w_uv[0], (1, 0, 2)).astype(BF16),
        "wout_att": w_out_att[0].astype(BF16),
        "lam_q1": lam_q1, "lam_k1": lam_k1, "lam_q2": lam_q2, "lam_k2": lam_k2,
        "g_subln": g_subln,
        "win_pool": w_in_pool[0].astype(BF16),
        "wgrp": w_grp_pool[0].astype(BF16),
        "scale_pool": scale_pool,
        "wout_pool": w_out_pool[0].astype(BF16),
        "g_final": g_final.reshape(1, D_MODEL),
        "pool_prefix": jnp.pad(state_pool[0], ((0, 0), (1, 0), (0, 0))),
    }

    cos_p, sin_p = _rope_tables(jnp.arange(s_p, dtype=jnp.int32))
    cos_s, sin_s = _rope_tables(past + jnp.arange(n_s, dtype=jnp.int32))
    cos_s = jnp.tile(cos_s, (SAMPLE_TILE_SEQS, 1))
    sin_s = jnp.tile(sin_s, (SAMPLE_TILE_SEQS, 1))

    out_p = _group(x_prompt, mod, cos_p, sin_p, False, None, w, 256, 256)
    cache = (cache_diff_k[0], cache_diff_v[0], cache_mla_ckv[0], cache_mla_kpe[0])
    out_s = _group(x_sample, mod, cos_s, sin_s, True, cache, w, 0, 0)
    return (out_p[0], out_s[0]) + out_p[1:] + out_s[1:]
```

```python
import functools
import math

import jax
import jax.numpy as jnp
from jax import lax
from jax.experimental import pallas as pl
from jax.experimental.pallas import tpu as pltpu

F32 = jnp.float32
BF16 = jnp.bfloat16

D_MODEL = 2048
CHUNK = 64
CHUNK_SHIFT = 6
H_A = 8
DQK = 64
DQK_SHIFT = 6
DV_A = 2 * DQK
W_A = H_A * DV_A
H_B = 8
Q_LORA = 512
KV_LORA = 512
NOPE_DIM = 128
ROPE_DIM = 64
V_DIM = 128
W_B = H_B * V_DIM
ROPE_THETA = 10000.0
W_ATT = W_A + W_B
POOL_WINDOWS = (2, 4, 8, 16)
N_POOL_GROUPS = 4
W_C = D_MODEL
POOL_GW = W_C // N_POOL_GROUPS
POOL_STATE = max(POOL_WINDOWS) - 1
POOL_PREFIX = POOL_STATE + 1
EPS = 1e-6
SUBLN_EPS = 1e-5
NEG_INF = -1e30
LOG2E = math.log2(math.e)
DIFF_QSCALE = DQK ** -0.5 * LOG2E
MLA_QSCALE = (NOPE_DIM + ROPE_DIM) ** -0.5 * LOG2E
MOD_ROWS = 32
FLASH_DIFF_HEADS = 1
FLASH_MLA_HEADS = 2
SAMPLE_TILE_SEQS = 8
VMEM_LIMIT = 56 * 1024 * 1024


def _lambda_init(layer_idx):
    return 0.8 - 0.6 * math.exp(-0.3 * layer_idx)


def _rms(x, eps):
    return x * lax.rsqrt(jnp.mean(x * x, axis=-1, keepdims=True) + eps)


def _silu(x):
    return x * (1.0 / (1.0 + jnp.exp(-x)))


def _dot(a, b):
    return jnp.dot(a, b, preferred_element_type=F32)


def _dot_nt(a, b):
    return lax.dot_general(a, b, (((1,), (1,)), ((), ())), preferred_element_type=F32)


def _resident(shape):
    nd = len(shape)
    return pl.BlockSpec(shape, lambda *_: (0,) * nd, pipeline_mode=pl.Buffered(1))


def _params(*sem):
    return pltpu.CompilerParams(dimension_semantics=sem, vmem_limit_bytes=VMEM_LIMIT)


def _ada_kernel(c_ref, w_ref, b_ref, o_ref):
    cond = _silu(c_ref[...]).astype(BF16)
    w = w_ref[0].astype(BF16)
    o_ref[0, 0] = _dot(cond, w) + b_ref[0, 0]


def _ada(c_all, w_ada, b_ada):
    depth = w_ada.shape[0]
    tn = 1024
    per = D_MODEL // tn
    return pl.pallas_call(
        _ada_kernel,
        grid=(depth, 3 * per),
        in_specs=[
            pl.BlockSpec((MOD_ROWS, D_MODEL), lambda l, j: (0, 0)),
            pl.BlockSpec((1, D_MODEL, tn), lambda l, j: (l, 0, j)),
            pl.BlockSpec((1, 1, 1, tn), lambda l, j: (l, j // per, 0, j % per)),
        ],
        out_specs=pl.BlockSpec((1, 1, MOD_ROWS, tn), lambda l, j: (l, j // per, 0, j % per)),
        out_shape=jax.ShapeDtypeStruct((depth, 3, MOD_ROWS, D_MODEL), F32),
        compiler_params=_params("arbitrary", "arbitrary"),
        name="ada",
    )(c_all, w_ada, b_ada.reshape(depth, 3, 1, D_MODEL))


def _prenorm(x, gn, scale, shift):
    return _rms(x, EPS) * gn * (1.0 + scale) + shift


def _in_qkv_kernel(x_ref, shift_ref, scale_ref, gn_ref, wqkv_ref, wgate_ref,
                   qa_ref, k32_ref, v32_ref, sg_ref, *flash_refs):
    nb, rows, _ = x_ref.shape
    h = _prenorm(x_ref[...], gn_ref[...], scale_ref[...], shift_ref[...])
    hb = h.reshape(nb * rows, D_MODEL).astype(BF16)
    z = _dot_nt(hb, wqkv_ref[...])
    qa_ref[...] = (z[:, :W_A] * DIFF_QSCALE).astype(BF16)
    ka = z[:, W_A:2 * W_A]
    va = z[:, 2 * W_A:]
    for hd in range(H_A):
        k32_ref[:, hd, :] = ka[:, hd * DV_A:(hd + 1) * DV_A]
        v32_ref[:, hd, :] = va[:, hd * DV_A:(hd + 1) * DV_A]
    if flash_refs:
        k16_ref, vt_ref = flash_refs
        k16_ref[...] = ka.astype(BF16)
        vt_ref[0, 0] = va.T.astype(BF16)
    sg_ref[...] = _silu(_dot_nt(hb, wgate_ref[...])).astype(BF16)


def _latent_common(x_ref, shift_ref, scale_ref, gn_ref, cos_ref, sin_ref, wc_ref, wkpe_ref,
                   gqa_ref, gkva_ref, wuq_ref, wuqr_ref, wuqrr_ref, ckv32_ref, kpe32_ref):
    nb, rows, _ = x_ref.shape
    h = _prenorm(x_ref[...], gn_ref[...], scale_ref[...], shift_ref[...])
    hb = h.reshape(nb * rows, D_MODEL).astype(BF16)
    zc = _dot_nt(hb, wc_ref[...])
    cq = (_rms(zc[:, :Q_LORA], EPS) * gqa_ref[...]).astype(BF16)
    ckv = _rms(zc[:, Q_LORA:], EPS) * gkva_ref[...]
    ckv32_ref[...] = ckv
    cos = cos_ref[...]
    sin = sin_ref[...]
    zk = _dot_nt(hb, wkpe_ref[...])
    kpe_pad = zk[:, :128] * cos + zk[:, 128:] * sin
    kpe32_ref[...] = kpe_pad[:, :ROPE_DIM]
    q_nope = _dot(cq, wuq_ref[...]) * MLA_QSCALE
    cos8 = jnp.tile(cos, (1, H_B))
    sin8 = jnp.tile(sin, (1, H_B))
    q_pe = (_dot(cq, wuqr_ref[...]) * cos8 + _dot(cq, wuqrr_ref[...]) * sin8) * MLA_QSCALE
    return ckv, kpe_pad, q_nope, q_pe


def _in_lat_prompt_kernel(x_ref, shift_ref, scale_ref, gn_ref, cos_ref, sin_ref, wc_ref, wkpe_ref,
                          gqa_ref, gkva_ref, wuq_ref, wuqr_ref, wuqrr_ref, wuk_ref, wuv_ref,
                          ckv32_ref, kpe32_ref, qcat_ref, kcat_ref, vmla_ref):
    ckv, kpe_pad, q_nope, q_pe = _latent_common(
        x_ref, shift_ref, scale_ref, gn_ref, cos_ref, sin_ref, wc_ref, wkpe_ref,
        gqa_ref, gkva_ref, wuq_ref, wuqr_ref, wuqrr_ref, ckv32_ref, kpe32_ref)
    ckvb = ckv.astype(BF16)
    k_nope = _dot(ckvb, wuk_ref[...])
    vmla_ref[0, 0] = _dot(ckvb, wuv_ref[...]).T.astype(BF16)
    kpe_b = kpe_pad.astype(BF16)
    for hd in range(H_B):
        sl = slice(hd * 128, (hd + 1) * 128)
        qcat_ref[:, hd * 256:hd * 256 + 128] = q_nope[:, sl].astype(BF16)
        qcat_ref[:, hd * 256 + 128:(hd + 1) * 256] = q_pe[:, sl].astype(BF16)
        kcat_ref[:, hd * 256:hd * 256 + 128] = k_nope[:, sl].astype(BF16)
        kcat_ref[:, hd * 256 + 128:(hd + 1) * 256] = kpe_b


def _in_lat_sample_kernel(x_ref, shift_ref, scale_ref, gn_ref, cos_ref, sin_ref, wc_ref, wkpe_ref,
                          gqa_ref, gkva_ref, wuq_ref, wuqr_ref, wuqrr_ref, wukt_ref,
                          ckv32_ref, kpe32_ref, qlat_ref, qpe_ref):
    _, _, q_nope, q_pe = _latent_common(
        x_ref, shift_ref, scale_ref, gn_ref, cos_ref, sin_ref, wc_ref, wkpe_ref,
        gqa_ref, gkva_ref, wuq_ref, wuqr_ref, wuqrr_ref, ckv32_ref, kpe32_ref)
    qn = q_nope.astype(BF16)
    for hd in range(H_B):
        qlat_ref[:, hd * KV_LORA:(hd + 1) * KV_LORA] = _dot(
            qn[:, hd * 128:(hd + 1) * 128], wukt_ref[hd]).astype(BF16)
    qpe_ref[...] = q_pe.astype(BF16)


def _online_softmax_step(s, v, m, l, acc):
    m_new = jnp.maximum(m, jnp.max(s, axis=1, keepdims=True))
    alpha = jnp.exp2(m - m_new)
    p = jnp.exp2(s - m_new)
    l = alpha * l + jnp.sum(p, axis=1, keepdims=True)
    acc = alpha * acc + _dot(p.astype(BF16), v)
    return m_new, l, acc


def _flash_cols(qs, k_ref, vt_ref, s_ref, acc_ref, dk, dv, q_tile, tq, tk, pos_of_col):
    cols = qs[0].shape[0]
    assert tq == 2 * tk
    heads = range(len(qs))
    q_chunk = pos_of_col(lax.broadcasted_iota(jnp.int32, (tk, cols), 1)) >> CHUNK_SHIFT
    k_row = lax.broadcasted_iota(jnp.int32, (tk, cols), 0)

    def diag_mask(d):
        return ((k_row + d * tk) >> CHUNK_SHIFT) <= q_chunk

    def scores(tile, slot, mask):
        off = pl.multiple_of(tile * tk, tk)
        col_max = []
        for g in heads:
            s = _dot_nt(k_ref[0, pl.ds(off, tk), g * dk:(g + 1) * dk], qs[g])
            if mask is not None:
                s = jnp.where(mask, s, NEG_INF)
            s_ref[slot, g] = s
            col_max.append(jnp.max(s, axis=0, keepdims=True))
        return tuple(col_max)

    def consume(tile, slot, col_max, ml):
        out = []
        for g in heads:
            m, l = ml[g]
            m_new = jnp.maximum(m, col_max[g])
            alpha = jnp.exp2(m - m_new)
            p = jnp.exp2(s_ref[slot, g] - m_new)
            l = alpha * l + jnp.sum(p, axis=0, keepdims=True)
            acc_ref[g] = alpha * acc_ref[g] + _dot(vt_ref[0, tile, g * dv:(g + 1) * dv, :], p.astype(BF16))
            out.append((m_new, l))
        return tuple(out)

    def pair(i, carry, next_mask):
        cm0, ml = carry
        cm1 = scores(2 * i + 1, 1, None)
        ml = consume(2 * i, 0, cm0, ml)
        cm0 = scores(2 * i + 2, 0, next_mask)
        ml = consume(2 * i + 1, 1, cm1, ml)
        return cm0, ml

    acc_ref[...] = jnp.zeros(acc_ref.shape, F32)
    ml = ((jnp.full((1, cols), NEG_INF, F32), jnp.zeros((1, cols), F32)),) * len(qs)
    cm0 = scores(0, 0, jnp.logical_or(q_tile > 0, diag_mask(0)))
    carry = lax.fori_loop(0, jnp.maximum(q_tile - 1, 0), lambda i, c: pair(i, c, None), (cm0, ml))
    carry = lax.cond(q_tile > 0, lambda c: pair(q_tile - 1, c, diag_mask(0)), lambda c: c, carry)
    cm0, ml = carry
    cm1 = scores(2 * q_tile + 1, 1, diag_mask(1))
    ml = consume(2 * q_tile, 0, cm0, ml)
    ml = consume(2 * q_tile + 1, 1, cm1, ml)
    return [(acc_ref[g], ml[g][1]) for g in heads]


def _lambda_full(lq1_ref, lk1_ref, lq2_ref, lk2_ref, lam_init):
    a = jnp.sum(lq1_ref[...] * lk1_ref[...], axis=-1, keepdims=True)
    b = jnp.sum(lq2_ref[...] * lk2_ref[...], axis=-1, keepdims=True)
    return jnp.exp(a) - jnp.exp(b) + lam_init


def _flash_diff_kernel(tq, tk, lam_init, q_ref, k_ref, vt_ref, lq1_ref, lk1_ref, lq2_ref, lk2_ref,
                       gsub_ref, o_ref, s_ref, acc_ref):
    heads = q_ref.shape[2] // DV_A
    qs = []
    for g in range(heads):
        q = q_ref[0, :, g * DV_A:(g + 1) * DV_A]
        lane = lax.broadcasted_iota(jnp.int32, q.shape, 1)
        zero = jnp.zeros_like(q)
        qs.append(jnp.concatenate(
            [jnp.where(lane < DQK, q, zero), jnp.where(lane >= DQK, q, zero)], axis=0))
    res = _flash_cols(qs, k_ref, vt_ref, s_ref, acc_ref, DV_A, DV_A, pl.program_id(2), tq, tk,
                      lambda r: jnp.where(r >= tq, r - tq, r))
    lam = _lambda_full(lq1_ref, lk1_ref, lq2_ref, lk2_ref, lam_init)
    for g, (acc, l) in enumerate(res):
        o = acc / l
        o = o[:, :tq] - lam * o[:, tq:]
        o = o * lax.rsqrt(jnp.mean(o * o, axis=0, keepdims=True) + SUBLN_EPS)
        o_ref[0, :, g * DV_A:(g + 1) * DV_A] = (
            o * gsub_ref[...] * (1.0 - lam_init)).T.astype(o_ref.dtype)


def _flash_mla_kernel(tq, tk, q_ref, k_ref, vt_ref, o_ref, s_ref, acc_ref):
    heads = q_ref.shape[2] // 256
    qs = [q_ref[0, :, g * 256:(g + 1) * 256] for g in range(heads)]
    res = _flash_cols(qs, k_ref, vt_ref, s_ref, acc_ref, 256, V_DIM, pl.program_id(2), tq, tk,
                      lambda r: r)
    for g, (acc, l) in enumerate(res):
        o_ref[0, :, g * V_DIM:(g + 1) * V_DIM] = (acc / l).T.astype(o_ref.dtype)


def _flash_diff(qa, k16, v16t, lam_refs, gsub_col, lam_init, b, s, tk):
    tq = 2 * tk
    heads = FLASH_DIFF_HEADS
    gw = heads * DV_A
    vec = pl.BlockSpec((1, DQK), lambda bi, hi, qi: (0, 0))
    return pl.pallas_call(
        functools.partial(_flash_diff_kernel, tq, tk, lam_init),
        grid=(b, H_A // heads, s // tq),
        in_specs=[
            pl.BlockSpec((1, tq, gw), lambda bi, hi, qi: (bi, qi, hi)),
            pl.BlockSpec((1, s, gw), lambda bi, hi, qi: (bi, 0, hi)),
            pl.BlockSpec((1, s // tk, gw, tk), lambda bi, hi, qi: (bi, 0, hi, 0)),
            vec, vec, vec, vec,
            pl.BlockSpec((DV_A, 1), lambda bi, hi, qi: (0, 0)),
        ],
        out_specs=pl.BlockSpec((1, tq, gw), lambda bi, hi, qi: (bi, qi, hi)),
        out_shape=jax.ShapeDtypeStruct((b, s, W_A), BF16),
        scratch_shapes=[pltpu.VMEM((2, heads, tk, 2 * tq), F32), pltpu.VMEM((heads, DV_A, 2 * tq), F32)],
        compiler_params=_params("arbitrary", "arbitrary", "arbitrary"),
        name="flash_diff",
    )(qa, k16, v16t, *lam_refs, gsub_col)


def _flash_mla(qcat, kcat, vmlat, b, s, tk):
    tq = 2 * tk
    g = FLASH_MLA_HEADS
    return pl.pallas_call(
        functools.partial(_flash_mla_kernel, tq, tk),
        grid=(b, H_B // g, s // tq),
        in_specs=[
            pl.BlockSpec((1, tq, g * 256), lambda bi, hi, qi: (bi, qi, hi)),
            pl.BlockSpec((1, s, g * 256), lambda bi, hi, qi: (bi, 0, hi)),
            pl.BlockSpec((1, s // tk, g * V_DIM, tk), lambda bi, hi, qi: (bi, 0, hi, 0)),
        ],
        out_specs=pl.BlockSpec((1, tq, g * V_DIM), lambda bi, hi, qi: (bi, qi, hi)),
        out_shape=jax.ShapeDtypeStruct((b, s, W_B), BF16),
        scratch_shapes=[pltpu.VMEM((2, g, tk, tq), F32), pltpu.VMEM((g, V_DIM, tq), F32)],
        compiler_params=_params("arbitrary", "arbitrary", "arbitrary"),
        name="flash_mla",
    )(qcat, kcat, vmlat)


def _dec_update(s, v, m_ref, l_ref, acc_ref, g):
    m_new, l, acc = _online_softmax_step(s, v, m_ref[g], l_ref[g], acc_ref[g])
    m_ref[g] = m_new
    l_ref[g] = l
    acc_ref[g] = acc


def _dec_diff_kernel(n, lam_init, q_ref, k_ref, v_ref, kn_ref, vn_ref, lq1_ref, lk1_ref, lq2_ref,
                     lk2_ref, gsub_ref, o_ref, qbd_ref, m_ref, l_ref, acc_ref):
    j = pl.program_id(1)
    gw = 4 * DV_A

    @pl.when(j == 0)
    def _():
        q = q_ref[...]
        for g in range(2):
            qg = q[:, g * gw:(g + 1) * gw]
            lane = lax.broadcasted_iota(jnp.int32, qg.shape, 1)
            blocks = [jnp.where((lane >> DQK_SHIFT) == blk, qg, jnp.zeros_like(qg))
                      for blk in range(8)]
            qbd_ref[g] = jnp.concatenate(blocks, axis=0)
        m_ref[...] = jnp.full(m_ref.shape, NEG_INF, F32)
        l_ref[...] = jnp.zeros(l_ref.shape, F32)
        acc_ref[...] = jnp.zeros(acc_ref.shape, F32)

    def group_lanes(ref, g):
        keys = ref.shape[0] // H_A
        return jnp.concatenate(
            [ref[pl.ds(g * 4 + hl, keys, stride=H_A), :] for hl in range(4)], axis=1).astype(BF16)

    def attend(kref, vref):
        scores = [_dot_nt(qbd_ref[g], group_lanes(kref, g)) for g in range(2)]
        for g in range(2):
            _dec_update(scores[g], group_lanes(vref, g), m_ref, l_ref, acc_ref, g)

    attend(k_ref.at[0], v_ref.at[0])

    @pl.when(j == pl.num_programs(1) - 1)
    def _():
        lam = _lambda_full(lq1_ref, lk1_ref, lq2_ref, lk2_ref, lam_init)
        attend(kn_ref, vn_ref)
        for g in range(2):
            o = acc_ref[g] / l_ref[g]
            for hl in range(4):
                cols = slice(hl * DV_A, (hl + 1) * DV_A)
                o0 = o[(2 * hl) * n:(2 * hl + 1) * n, cols]
                o1 = o[(2 * hl + 1) * n:(2 * hl + 2) * n, cols]
                od = _rms(o0 - lam * o1, SUBLN_EPS) * gsub_ref[...] * (1.0 - lam_init)
                hd = g * 4 + hl
                o_ref[:, hd * DV_A:(hd + 1) * DV_A] = od.astype(o_ref.dtype)


def _dec_diff(qa, cache_k, cache_v, k_new, v_new, lam_refs, gsub, lam_init, b, n):
    past = cache_k.shape[1] // H_A
    tk = 1024
    vec = pl.BlockSpec((1, DQK), lambda bi, j: (0, 0))
    rows = pl.BlockSpec((n, W_A), lambda bi, j: (bi, 0))
    new_rows = pl.BlockSpec((n * H_A, DV_A), lambda bi, j: (bi, 0))
    cache_tile = pl.BlockSpec((1, tk * H_A, DV_A), lambda bi, j: (bi, j, 0))
    return pl.pallas_call(
        functools.partial(_dec_diff_kernel, n, lam_init),
        grid=(b, past // tk),
        in_specs=[
            rows, cache_tile, cache_tile, new_rows, new_rows, vec, vec, vec, vec,
            pl.BlockSpec((1, DV_A), lambda bi, j: (0, 0)),
        ],
        out_specs=rows,
        out_shape=jax.ShapeDtypeStruct((b * n, W_A), BF16),
        scratch_shapes=[
            pltpu.VMEM((2, 8 * n, 4 * DV_A), BF16),
            pltpu.VMEM((2, 8 * n, 1), F32),
            pltpu.VMEM((2, 8 * n, 1), F32),
            pltpu.VMEM((2, 8 * n, 4 * DV_A), F32),
        ],
        compiler_params=_params("arbitrary", "arbitrary"),
        name="dec_diff",
    )(qa, cache_k, cache_v, k_new, v_new, *lam_refs, gsub)


def _dec_mla_kernel(n, qlat_ref, qpe_ref, ckv_ref, kpet_ref, ckvn_ref, kpen_ref, wuv_ref, o_ref,
                    ql_ref, qp_ref, m_ref, l_ref, acc_ref):
    j = pl.program_id(1)

    @pl.when(j == 0)
    def _():
        for hd in range(H_B):
            ql_ref[hd * n:(hd + 1) * n, :] = qlat_ref[:, hd * KV_LORA:(hd + 1) * KV_LORA]
            qp_ref[hd * n:(hd + 1) * n, :] = qpe_ref[:, hd * 128:(hd + 1) * 128]
        m_ref[...] = jnp.full(m_ref.shape, NEG_INF, F32)
        l_ref[...] = jnp.zeros(l_ref.shape, F32)
        acc_ref[...] = jnp.zeros(acc_ref.shape, F32)

    def update(ckv, pe_scores):
        ckv = ckv.astype(BF16)
        _dec_update(_dot_nt(ql_ref[...], ckv) + pe_scores, ckv, m_ref, l_ref, acc_ref, 0)

    qp = qp_ref[:, :ROPE_DIM]
    update(ckv_ref[0], _dot(qp, kpet_ref[0].astype(BF16)))

    @pl.when(j == pl.num_programs(1) - 1)
    def _():
        update(ckvn_ref[...], _dot_nt(qp, kpen_ref[...].astype(BF16)))
        o = (acc_ref[0] / l_ref[0]).astype(BF16)
        for hd in range(H_B):
            o_ref[:, hd * V_DIM:(hd + 1) * V_DIM] = _dot(
                o[hd * n:(hd + 1) * n], wuv_ref[hd]).astype(o_ref.dtype)


def _dec_mla(qlat, qpe, cache_ckv, cache_kpe_t, ckv_new, kpe_new, wuv_h, b, n):
    past = cache_ckv.shape[1]
    tk = 1024
    return pl.pallas_call(
        functools.partial(_dec_mla_kernel, n),
        grid=(b, past // tk),
        in_specs=[
            pl.BlockSpec((n, H_B * KV_LORA), lambda bi, j: (bi, 0)),
            pl.BlockSpec((n, H_B * 128), lambda bi, j: (bi, 0)),
            pl.BlockSpec((1, tk, KV_LORA), lambda bi, j: (bi, j, 0)),
            pl.BlockSpec((1, ROPE_DIM, tk), lambda bi, j: (bi, 0, j)),
            pl.BlockSpec((n, KV_LORA), lambda bi, j: (bi, 0)),
            pl.BlockSpec((n, ROPE_DIM), lambda bi, j: (bi, 0)),
            pl.BlockSpec((H_B, KV_LORA, V_DIM), lambda bi, j: (0, 0, 0)),
        ],
        out_specs=pl.BlockSpec((n, W_B), lambda bi, j: (bi, 0)),
        out_shape=jax.ShapeDtypeStruct((b * n, W_B), BF16),
        scratch_shapes=[
            pltpu.VMEM((H_B * n, KV_LORA), BF16),
            pltpu.VMEM((H_B * n, 128), BF16),
            pltpu.VMEM((1, H_B * n, 1), F32),
            pltpu.VMEM((1, H_B * n, 1), F32),
            pltpu.VMEM((1, H_B * n, KV_LORA), F32),
        ],
        compiler_params=_params("arbitrary", "arbitrary"),
        name="dec_mla",
    )(qlat, qpe, cache_ckv, cache_kpe_t, ckv_new, kpe_new, wuv_h)


def _out_proj_kernel(oa_ref, ob_ref, sg_ref, x_ref, gate_ref, w_ref, o_ref):
    nb, rows, _ = x_ref.shape
    sg = sg_ref[...].astype(F32)
    oa = (oa_ref[...].astype(F32) * sg[:, :W_A]).astype(BF16)
    ob = (ob_ref[...].astype(F32) * sg[:, W_A:]).astype(BF16)
    y = _dot(oa, w_ref[:W_A, :]) + _dot(ob, w_ref[W_A:, :])
    o_ref[...] = x_ref[...] + gate_ref[...] * y.reshape(nb, rows, D_MODEL)


def _pool_kernel(tiles_per_seq, pos0, carry, x_ref, shift_ref, scale_ref, gate_ref, gn_ref, pre_ref,
                 win_ref, wgrp_ref, sp_ref, wout_ref, gf_ref, y_ref, st_ref, uext_ref):
    nb, rows, _ = x_ref.shape
    tm = nb * rows
    t = pl.program_id(0) % tiles_per_seq
    x = x_ref[...]
    h = _prenorm(x, gn_ref[...], scale_ref[...], shift_ref[...])
    z = _dot(h.reshape(tm, D_MODEL).astype(BF16), win_ref[...])
    u = z[:, :W_C].reshape(nb, rows, W_C)
    sgate = _silu(z[:, W_C:])

    if carry:
        @pl.when(t == 0)
        def _():
            uext_ref[:, :POOL_PREFIX, :] = pre_ref[...]
    else:
        uext_ref[:, :POOL_PREFIX, :] = pre_ref[...]
    uext_ref[:, POOL_PREFIX:, :] = u

    pos = pos0 + t * rows + lax.broadcasted_iota(jnp.int32, (1, rows, 1), 1)
    parts = []
    for g, w in enumerate(POOL_WINDOWS):
        cols = slice(g * POOL_GW, (g + 1) * POOL_GW)
        ws = u[:, :, cols]
        for k in range(1, w):
            ws = ws + uext_ref[:, POOL_PREFIX - k:POOL_PREFIX - k + rows, cols]
        cnt = jnp.minimum(pos + 1, w).astype(F32)
        dgrp = (ws / cnt - u[:, :, cols]).reshape(tm, POOL_GW).astype(BF16)
        parts.append(_dot(dgrp, wgrp_ref[g]))
    m = jnp.concatenate(parts, axis=1) * sp_ref[...]
    y = _dot((m * sgate).astype(BF16), wout_ref[...])
    x2 = x + gate_ref[...] * y.reshape(nb, rows, D_MODEL)
    y_ref[...] = _rms(x2, EPS) * gf_ref[...]
    tail = uext_ref[:, rows:rows + POOL_PREFIX, :]
    st_ref[...] = tail
    if carry:
        uext_ref[:, :POOL_PREFIX, :] = tail


def _rot_half_cols(w):
    half = ROPE_DIM // 2
    return jnp.concatenate([-w[..., half:], w[..., :half]], axis=-1)


def _rope_tables(pos):
    half = ROPE_DIM // 2
    inv = ROPE_THETA ** (-jnp.arange(half, dtype=F32) / half)
    ang = pos.astype(F32)[:, None] * inv[None, :]
    cos, sin = jnp.cos(ang), jnp.sin(ang)
    zero = jnp.zeros((pos.shape[0], 128 - ROPE_DIM), F32)
    return jnp.concatenate([cos, cos, zero], axis=1), jnp.concatenate([sin, sin, zero], axis=1)


def _group(x3, mod, cos, sin, sample, cache, w, tm_att, tm_pool):
    bt, st, _ = x3.shape
    tokens = bt * st
    lam_init = _lambda_init(0)

    if sample:
        nb, rows, tps = SAMPLE_TILE_SEQS, st, 1
        n_tiles = bt // nb
        x_spec = pl.BlockSpec((nb, rows, D_MODEL), lambda i: (i, 0, 0))

        def mod_spec(layer, comp):
            blk = (layer * 3 + comp) * MOD_ROWS // nb
            return pl.BlockSpec((nb, 1, D_MODEL), lambda i: (blk + i, 0, 0))
    else:
        nb, rows = 1, tm_att
        tps = st // rows
        n_tiles = bt * tps
        x_spec = pl.BlockSpec((1, rows, D_MODEL), lambda i: (i // tps, i % tps, 0))

        def mod_spec(layer, comp):
            base = (layer * 3 + comp) * MOD_ROWS + 16
            return pl.BlockSpec((1, 1, D_MODEL), lambda i: (base + i // tps, 0, 0))

    tm = nb * rows

    def tok(width):
        return pl.BlockSpec((tm, width), lambda i: (i, 0))

    def tok_shape(width, dt):
        return jax.ShapeDtypeStruct((tokens, width), dt)

    vec = lambda width: pl.BlockSpec((1, width), lambda i: (0, 0))
    if sample:
        tab = pl.BlockSpec((tm, 128), lambda i: (0, 0))
    else:
        tab = pl.BlockSpec((tm, 128), lambda i: (i % tps, 0))

    heads_spec = pl.BlockSpec((tm, H_A, DV_A), lambda i: (i, 0, 0))
    heads_shape = jax.ShapeDtypeStruct((tokens, H_A, DV_A), F32)
    vt_spec = pl.BlockSpec((1, 1, W_A, tm), lambda i: (i // tps, i % tps, 0, 0))
    vt_shape = jax.ShapeDtypeStruct((bt, st // tm, W_A, tm), BF16)
    qkv_out = pl.pallas_call(
        _in_qkv_kernel,
        grid=(n_tiles,),
        in_specs=[x_spec, mod_spec(0, 0), mod_spec(0, 1), vec(D_MODEL),
                  _resident((3 * W_A, D_MODEL)), _resident((W_ATT, D_MODEL))],
        out_specs=[tok(W_A), heads_spec, heads_spec, tok(W_ATT)] + ([] if sample else [tok(W_A), vt_spec]),
        out_shape=[tok_shape(W_A, BF16), heads_shape, heads_shape, tok_shape(W_ATT, BF16)]
        + ([] if sample else [tok_shape(W_A, BF16), vt_shape]),
        compiler_params=_params("arbitrary"),
        name="in_qkv",
    )(x3, mod, mod, w["g_norm0"], w["wqkv"], w["wgate"])
    qa, k32, v32, sgate = qkv_out[:4]

    lat_in = [x_spec, mod_spec(0, 0), mod_spec(0, 1), vec(D_MODEL), tab, tab,
              _resident((Q_LORA + KV_LORA, D_MODEL)), _resident((256, D_MODEL)),
              vec(Q_LORA), vec(KV_LORA),
              _resident((Q_LORA, H_B * 128)), _resident((Q_LORA, H_B * 128)), _resident((Q_LORA, H_B * 128))]
    lat_args = [x3, mod, mod, w["g_norm0"], cos, sin, w["wc"], w["wkpe"], w["g_qa"], w["g_kva"],
                w["wuq"], w["wuqr"], w["wuqrr"]]
    lam_refs = (w["lam_q1"], w["lam_k1"], w["lam_q2"], w["lam_k2"])

    if sample:
        ckv32, kpe32, qlat, qpe = pl.pallas_call(
            _in_lat_sample_kernel,
            grid=(n_tiles,),
            in_specs=lat_in + [_resident((H_B, 128, KV_LORA))],
            out_specs=[tok(KV_LORA), tok(ROPE_DIM), tok(H_B * KV_LORA), tok(H_B * 128)],
            out_shape=[tok_shape(KV_LORA, F32), tok_shape(ROPE_DIM, F32),
                       tok_shape(H_B * KV_LORA, BF16), tok_shape(H_B * 128, BF16)],
            compiler_params=_params("arbitrary"),
            name="in_lat_sample",
        )(*lat_args, w["wukt"])
        cache_k, cache_v, cache_ckv, cache_kpe = cache
        o_a = _dec_diff(qa, cache_k, cache_v, k32.reshape(tokens * H_A, DV_A),
                        v32.reshape(tokens * H_A, DV_A), lam_refs, w["g_subln"], lam_init, bt, st)
        o_b = _dec_mla(qlat, qpe, cache_ckv, cache_kpe, ckv32, kpe32, w["wuv_h"], bt, st)
    else:
        ckv32, kpe32, qcat, kcat, vmla = pl.pallas_call(
            _in_lat_prompt_kernel,
            grid=(n_tiles,),
            in_specs=lat_in + [_resident((KV_LORA, H_B * 128)), _resident((KV_LORA, H_B * 128))],
            out_specs=[tok(KV_LORA), tok(ROPE_DIM), tok(H_B * 256), tok(H_B * 256), vt_spec],
            out_shape=[tok_shape(KV_LORA, F32), tok_shape(ROPE_DIM, F32),
                       tok_shape(H_B * 256, BF16), tok_shape(H_B * 256, BF16), vt_shape],
            compiler_params=_params("arbitrary"),
            name="in_lat_prompt",
        )(*lat_args, w["wuk"], w["wuv"])
        k16, v16t = qkv_out[4:]
        o_a = _flash_diff(qa.reshape(bt, st, W_A), k16.reshape(bt, st, W_A), v16t, lam_refs,
                          w["g_subln"].reshape(DV_A, 1), lam_init, bt, st, tm).reshape(tokens, W_A)
        o_b = _flash_mla(qcat.reshape(bt, st, H_B * 256), kcat.reshape(bt, st, H_B * 256),
                         vmla, bt, st, tm).reshape(tokens, W_B)

    x1 = pl.pallas_call(
        _out_proj_kernel,
        grid=(n_tiles,),
        in_specs=[tok(W_A), tok(W_B), tok(W_ATT), x_spec, mod_spec(0, 2), _resident((W_ATT, D_MODEL))],
        out_specs=x_spec,
        out_shape=jax.ShapeDtypeStruct(x3.shape, F32),
        compiler_params=_params("arbitrary"),
        name="out_proj",
    )(o_a, o_b, sgate, x3, mod, w["wout_att"])

    if sample:
        p_nb, p_rows, p_tiles, p_tps = nb, st, n_tiles, 1
        px_spec = x_spec
        pmod = mod_spec
        pre_spec = pl.BlockSpec((nb, POOL_PREFIX, W_C), lambda i: (i, 0, 0))
        st_spec = pl.BlockSpec((nb, POOL_PREFIX, W_C), lambda i: (i, 0, 0))
        pos0 = cache[2].shape[1]
        prefix = w["pool_prefix"]
    else:
        p_nb, p_rows = 1, tm_pool
        p_tps = st // p_rows
        p_tiles = bt * p_tps
        px_spec = pl.BlockSpec((1, p_rows, D_MODEL), lambda i: (i // p_tps, i % p_tps, 0))

        def pmod(layer, comp):
            base = (layer * 3 + comp) * MOD_ROWS + 16
            return pl.BlockSpec((1, 1, D_MODEL), lambda i: (base + i // p_tps, 0, 0))

        pre_spec = pl.BlockSpec((1, POOL_PREFIX, W_C), lambda i: (0, 0, 0))
        st_spec = pl.BlockSpec((1, POOL_PREFIX, W_C), lambda i: (i // p_tps, 0, 0))
        pos0 = 0
        prefix = jnp.zeros((1, POOL_PREFIX, W_C), F32)

    y, state = pl.pallas_call(
        functools.partial(_pool_kernel, p_tps, pos0, not sample),
        grid=(p_tiles,),
        in_specs=[px_spec, pmod(1, 0), pmod(1, 1), pmod(1, 2), vec(D_MODEL), pre_spec,
                  _resident((D_MODEL, 2 * W_C)), _resident((N_POOL_GROUPS, POOL_GW, POOL_GW)),
                  vec(W_C), _resident((W_C, D_MODEL)), vec(D_MODEL)],
        out_specs=[px_spec, st_spec],
        out_shape=[jax.ShapeDtypeStruct(x3.shape, F32),
                   jax.ShapeDtypeStruct((bt, POOL_PREFIX, W_C), F32)],
        scratch_shapes=[pltpu.VMEM((p_nb, POOL_PREFIX + p_rows, W_C), F32)],
        compiler_params=_params("arbitrary"),
        name="pool",
    )(x1, mod, mod, mod, w["g_norm1"], prefix, w["win_pool"], w["wgrp"], w["scale_pool"],
      w["wout_pool"], w["g_final"])

    return (y,
            k32.reshape(1, bt, st, H_A, DV_A), v32.reshape(1, bt, st, H_A, DV_A),
            ckv32.reshape(1, bt, st, KV_LORA), kpe32.reshape(1, bt, st, ROPE_DIM),
            state[None, :, 1:, :])


def kernel(x_prompt, x_sample, c_prompt, c_sample, cache_diff_k, cache_diff_v, cache_mla_ckv, cache_mla_kpe, state_pool, g_norm, w_ada, b_ada, w_in_att, w_out_att, lam_q1, lam_k1, lam_q2, lam_k2, g_subln, g_qa, w_uq, w_uqr, w_uk, w_uv, g_kva, w_in_pool, w_grp_pool, scale_pool, w_out_pool, g_final):
    b_p, s_p, _ = x_prompt.shape
    b_s, n_s, _ = x_sample.shape
    past = cache_diff_k.shape[2]

    c_all = jnp.concatenate(
        [c_sample, c_prompt, jnp.zeros((MOD_ROWS - b_s - b_p, D_MODEL), F32)], axis=0)
    mod = _ada(c_all, w_ada, b_ada).reshape(-1, 1, D_MODEL)

    w_in_t = w_in_att[0].T
    kpe_w = w_in_t[4 * W_A:4 * W_A + ROPE_DIM].T
    zpad = jnp.zeros((D_MODEL, 128 - ROPE_DIM), F32)
    uqr = w_uqr[0]
    pad_heads = lambda a: jnp.pad(a, ((0, 0), (0, 0), (0, 128 - ROPE_DIM))).reshape(Q_LORA, H_B * 128)
    w = {
        "g_norm0": g_norm[0:1], "g_norm1": g_norm[1:2],
        "wqkv": w_in_t[:3 * W_A].astype(BF16),
        "wc": w_in_t[3 * W_A:4 * W_A].astype(BF16),
        "wkpe": jnp.concatenate([kpe_w, zpad, _rot_half_cols(kpe_w), zpad], axis=1).T.astype(BF16),
        "wgate": w_in_t[4 * W_A + ROPE_DIM:].astype(BF16),
        "g_qa": g_qa, "g_kva": g_kva,
        "wuq": w_uq[0].reshape(Q_LORA, H_B * NOPE_DIM).astype(BF16),
        "wuqr": pad_heads(uqr).astype(BF16),
        "wuqrr": pad_heads(_rot_half_cols(uqr)).astype(BF16),
        "wuk": w_uk[0].reshape(KV_LORA, H_B * NOPE_DIM).astype(BF16),
        "wuv": w_uv[0].reshape(KV_LORA, H_B * V_DIM).astype(BF16),
        "wukt": jnp.transpose(w_uk[0], (1, 2, 0)).astype(BF16),
        "wuv_h": jnp.transpose(w_uv[0], (1, 0, 2)).astype(BF16),
        "wout_att": w_out_att[0].astype(BF16),
        "lam_q1": lam_q1, "lam_k1": lam_k1, "lam_q2": lam_q2, "lam_k2": lam_k2,
        "g_subln": g_subln,
        "win_pool": w_in_pool[0].astype(BF16),
        "wgrp": w_grp_pool[0].astype(BF16),
        "scale_pool": scale_pool,
        "wout_pool": w_out_pool[0].astype(BF16),
        "g_final": g_final.reshape(1, D_MODEL),
        "pool_prefix": jnp.pad(state_pool[0], ((0, 0), (1, 0), (0, 0))),
    }

    cos_p, sin_p = _rope_tables(jnp.arange(s_p, dtype=jnp.int32))
    cos_s, sin_s = _rope_tables(past + jnp.arange(n_s, dtype=jnp.int32))
    cos_s = jnp.tile(cos_s, (SAMPLE_TILE_SEQS, 1))
    sin_s = jnp.tile(sin_s, (SAMPLE_TILE_SEQS, 1))

    out_p = _group(x_prompt, mod, cos_p, sin_p, False, None, w, 256, 256)
    cache = (cache_diff_k[0].reshape(b_s, past * H_A, DV_A), cache_diff_v[0].reshape(b_s, past * H_A, DV_A),
             cache_mla_ckv[0], jnp.swapaxes(cache_mla_kpe[0], 1, 2))
    out_s = _group(x_sample, mod, cos_s, sin_s, True, cache, w, 0, 0)
    return (out_p[0], out_s[0]) + out_p[1:] + out_s[1:]
```

```python
import functools
import math

import jax
import jax.numpy as jnp
from jax import lax
from jax.experimental import pallas as pl
from jax.experimental.pallas import tpu as pltpu

F32 = jnp.float32
BF16 = jnp.bfloat16

D_MODEL = 2048
CHUNK = 64
CHUNK_SHIFT = 6
H_A = 8
DQK = 64
DQK_SHIFT = 6
DV_A = 2 * DQK
W_A = H_A * DV_A
H_B = 8
Q_LORA = 512
KV_LORA = 512
NOPE_DIM = 128
ROPE_DIM = 64
V_DIM = 128
W_B = H_B * V_DIM
ROPE_THETA = 10000.0
W_ATT = W_A + W_B
POOL_WINDOWS = (2, 4, 8, 16)
N_POOL_GROUPS = 4
W_C = D_MODEL
POOL_GW = W_C // N_POOL_GROUPS
POOL_STATE = max(POOL_WINDOWS) - 1
POOL_PREFIX = POOL_STATE + 1
EPS = 1e-6
SUBLN_EPS = 1e-5
NEG_INF = -1e30
LOG2E = math.log2(math.e)
DIFF_QSCALE = DQK ** -0.5 * LOG2E
MLA_QSCALE = (NOPE_DIM + ROPE_DIM) ** -0.5 * LOG2E
MOD_ROWS = 32
FLASH_DIFF_HEADS = 2
FLASH_MLA_HEADS = 2
CHAIN_ROWS = 256
ONES_ROWS = 16
SAMPLE_TILE_SEQS = 8
VMEM_LIMIT = 56 * 1024 * 1024


def _lambda_init(layer_idx):
    return 0.8 - 0.6 * math.exp(-0.3 * layer_idx)


def _rms(x, eps):
    return x * lax.rsqrt(jnp.mean(x * x, axis=-1, keepdims=True) + eps)


def _silu(x):
    return x * (1.0 / (1.0 + jnp.exp(-x)))


def _dot(a, b):
    return jnp.dot(a, b, preferred_element_type=F32)


def _dot_nt(a, b):
    return lax.dot_general(a, b, (((1,), (1,)), ((), ())), preferred_element_type=F32)


def _resident(shape):
    nd = len(shape)
    return pl.BlockSpec(shape, lambda *_: (0,) * nd, pipeline_mode=pl.Buffered(1))


def _params(*sem):
    return pltpu.CompilerParams(dimension_semantics=sem, vmem_limit_bytes=VMEM_LIMIT)


def _ada_kernel(c_ref, w_ref, b_ref, o_ref):
    cond = _silu(c_ref[...]).astype(BF16)
    w = w_ref[0].astype(BF16)
    o_ref[0, 0] = _dot(cond, w) + b_ref[0, 0]


def _ada(c_all, w_ada, b_ada):
    depth = w_ada.shape[0]
    tn = 1024
    per = D_MODEL // tn
    return pl.pallas_call(
        _ada_kernel,
        grid=(depth, 3 * per),
        in_specs=[
            pl.BlockSpec((MOD_ROWS, D_MODEL), lambda l, j: (0, 0)),
            pl.BlockSpec((1, D_MODEL, tn), lambda l, j: (l, 0, j)),
            pl.BlockSpec((1, 1, 1, tn), lambda l, j: (l, j // per, 0, j % per)),
        ],
        out_specs=pl.BlockSpec((1, 1, MOD_ROWS, tn), lambda l, j: (l, j // per, 0, j % per)),
        out_shape=jax.ShapeDtypeStruct((depth, 3, MOD_ROWS, D_MODEL), F32),
        compiler_params=_params("arbitrary", "arbitrary"),
        name="ada",
    )(c_all, w_ada, b_ada.reshape(depth, 3, 1, D_MODEL))


def _row_chains(rows):
    step = min(rows, CHAIN_ROWS)
    return [(r, r + step) for r in range(0, rows, step)]


def _prenorm(x, gn, scale, shift):
    return _rms(x, EPS) * gn * (1.0 + scale) + shift


def _store_vt(vt_ref, tile, v, heads, dv):
    vt = v.T.astype(BF16)
    ones = jnp.ones((ONES_ROWS, v.shape[0]), BF16)
    for hd in range(heads):
        base = hd * (dv + ONES_ROWS)
        vt_ref[0, tile, base:base + dv, :] = vt[hd * dv:(hd + 1) * dv]
        vt_ref[0, tile, base + dv:base + dv + ONES_ROWS, :] = ones


def _token_chains(x_ref):
    nb, rows, _ = x_ref.shape
    chains = _row_chains(rows)
    assert nb == 1 or len(chains) == 1
    return [(c, slice(r0, r1), slice(nb * r0, nb * r1)) for c, (r0, r1) in enumerate(chains)]


def _prenorm_tokens(x_ref, rs, gn_ref, scale_ref, shift_ref):
    x = x_ref[:, rs, :]
    h = _prenorm(x, gn_ref[...], scale_ref[...], shift_ref[...])
    return h.reshape(x.shape[0] * x.shape[1], D_MODEL).astype(BF16)


def _in_qkv_kernel(x_ref, shift_ref, scale_ref, gn_ref, wqkv_ref, qa_ref, k32_ref, v32_ref, *flash_refs):
    for c, rs, ts in _token_chains(x_ref):
        z = _dot_nt(_prenorm_tokens(x_ref, rs, gn_ref, scale_ref, shift_ref), wqkv_ref[...])
        qa_ref[ts, :] = (z[:, :W_A] * DIFF_QSCALE).astype(BF16)
        ka = z[:, W_A:2 * W_A]
        va = z[:, 2 * W_A:]
        for hd in range(H_A):
            head_rows = pl.ds(ts.start * H_A + hd, ts.stop - ts.start, stride=H_A)
            k32_ref[head_rows, :] = ka[:, hd * DV_A:(hd + 1) * DV_A]
            v32_ref[head_rows, :] = va[:, hd * DV_A:(hd + 1) * DV_A]
        if flash_refs:
            k16_ref, vt_ref = flash_refs
            k16_ref[ts, :] = ka.astype(BF16)
            _store_vt(vt_ref, c, va, H_A, DV_A)


def _latent_chain(x_ref, rs, ts, shift_ref, scale_ref, gn_ref, cos_ref, sin_ref, wc_ref, wkpe_ref,
                  gqa_ref, gkva_ref, wuq_ref, wuqr_ref, wuqrr_ref, ckv32_ref, kpe32_ref):
    hb = _prenorm_tokens(x_ref, rs, gn_ref, scale_ref, shift_ref)
    zc = _dot_nt(hb, wc_ref[...])
    cq = (_rms(zc[:, :Q_LORA], EPS) * gqa_ref[...]).astype(BF16)
    ckv = _rms(zc[:, Q_LORA:], EPS) * gkva_ref[...]
    ckv32_ref[ts, :] = ckv
    cos = cos_ref[ts, :]
    sin = sin_ref[ts, :]
    zk = _dot_nt(hb, wkpe_ref[...])
    kpe_pad = zk[:, :128] * cos + zk[:, 128:] * sin
    kpe32_ref[ts, :] = kpe_pad[:, :ROPE_DIM]
    q_nope = _dot(cq, wuq_ref[...]) * MLA_QSCALE
    cos8 = jnp.tile(cos, (1, H_B))
    sin8 = jnp.tile(sin, (1, H_B))
    q_pe = (_dot(cq, wuqr_ref[...]) * cos8 + _dot(cq, wuqrr_ref[...]) * sin8) * MLA_QSCALE
    return ckv, kpe_pad, q_nope, q_pe


def _in_lat_prompt_kernel(x_ref, shift_ref, scale_ref, gn_ref, cos_ref, sin_ref, wc_ref, wkpe_ref,
                          gqa_ref, gkva_ref, wuq_ref, wuqr_ref, wuqrr_ref, wuk_ref, wuv_ref,
                          ckv32_ref, kpe32_ref, qcat_ref, kcat_ref, vmla_ref):
    for c, rs, ts in _token_chains(x_ref):
        ckv, kpe_pad, q_nope, q_pe = _latent_chain(
            x_ref, rs, ts, shift_ref, scale_ref, gn_ref, cos_ref, sin_ref, wc_ref, wkpe_ref,
            gqa_ref, gkva_ref, wuq_ref, wuqr_ref, wuqrr_ref, ckv32_ref, kpe32_ref)
        ckvb = ckv.astype(BF16)
        k_nope = _dot(ckvb, wuk_ref[...])
        _store_vt(vmla_ref, c, _dot(ckvb, wuv_ref[...]), H_B, V_DIM)
        kpe_b = kpe_pad.astype(BF16)
        for hd in range(H_B):
            sl = slice(hd * 128, (hd + 1) * 128)
            qcat_ref[ts, hd * 256:hd * 256 + 128] = q_nope[:, sl].astype(BF16)
            qcat_ref[ts, hd * 256 + 128:(hd + 1) * 256] = q_pe[:, sl].astype(BF16)
            kcat_ref[ts, hd * 256:hd * 256 + 128] = k_nope[:, sl].astype(BF16)
            kcat_ref[ts, hd * 256 + 128:(hd + 1) * 256] = kpe_b


def _in_lat_sample_kernel(x_ref, shift_ref, scale_ref, gn_ref, cos_ref, sin_ref, wc_ref, wkpe_ref,
                          gqa_ref, gkva_ref, wuq_ref, wuqr_ref, wuqrr_ref, wukt_ref,
                          ckv32_ref, kpe32_ref, qlat_ref, qpe_ref):
    for _, rs, ts in _token_chains(x_ref):
        _, _, q_nope, q_pe = _latent_chain(
            x_ref, rs, ts, shift_ref, scale_ref, gn_ref, cos_ref, sin_ref, wc_ref, wkpe_ref,
            gqa_ref, gkva_ref, wuq_ref, wuqr_ref, wuqrr_ref, ckv32_ref, kpe32_ref)
        qn = q_nope.astype(BF16)
        for hd in range(H_B):
            qlat_ref[ts, hd * KV_LORA:(hd + 1) * KV_LORA] = _dot(
                qn[:, hd * 128:(hd + 1) * 128], wukt_ref[hd]).astype(BF16)
        qpe_ref[ts, :] = q_pe.astype(BF16)


def _online_softmax_step(s, v, m, l, acc):
    m_new = jnp.maximum(m, jnp.max(s, axis=1, keepdims=True))
    alpha = jnp.exp2(m - m_new)
    p = jnp.exp2(s - m_new)
    l = alpha * l + jnp.sum(p, axis=1, keepdims=True)
    acc = alpha * acc + _dot(p.astype(BF16), v)
    return m_new, l, acc


def _flash_cols(qs, k_ref, vt_ref, s_ref, acc_ref, dk, dv, q_tile, tq, tk, pos_of_col):
    cols = qs[0].shape[0]
    dve = dv + ONES_ROWS
    assert tq == 2 * tk
    heads = range(len(qs))
    q_chunk = pos_of_col(lax.broadcasted_iota(jnp.int32, (tk, cols), 1)) >> CHUNK_SHIFT
    k_row = lax.broadcasted_iota(jnp.int32, (tk, cols), 0)

    def diag_mask(d):
        return ((k_row + d * tk) >> CHUNK_SHIFT) <= q_chunk

    def scores(tile, slot, mask):
        off = pl.multiple_of(tile * tk, tk)
        col_max = []
        for g in heads:
            s = _dot_nt(k_ref[0, pl.ds(off, tk), g * dk:(g + 1) * dk], qs[g])
            if mask is not None:
                s = jnp.where(mask, s, NEG_INF)
            s_ref[slot, g] = s
            col_max.append(jnp.max(s, axis=0, keepdims=True))
        return tuple(col_max)

    def consume(tile, slot, col_max, ms):
        out = []
        for g in heads:
            m_new = jnp.maximum(ms[g], col_max[g])
            alpha = jnp.exp2(ms[g] - m_new)
            p = jnp.exp2(s_ref[slot, g] - m_new).astype(BF16)
            acc_ref[g] = alpha * acc_ref[g] + _dot(vt_ref[0, tile, g * dve:(g + 1) * dve, :], p)
            out.append(m_new)
        return tuple(out)

    def pair(i, carry, next_mask):
        cm0, ms = carry
        cm1 = scores(2 * i + 1, 1, None)
        ms = consume(2 * i, 0, cm0, ms)
        cm0 = scores(2 * i + 2, 0, next_mask)
        ms = consume(2 * i + 1, 1, cm1, ms)
        return cm0, ms

    acc_ref[...] = jnp.zeros(acc_ref.shape, F32)
    ms = (jnp.full((1, cols), NEG_INF, F32),) * len(qs)
    cm0 = scores(0, 0, jnp.logical_or(q_tile > 0, diag_mask(0)))
    carry = lax.fori_loop(0, jnp.maximum(q_tile - 1, 0), lambda i, c: pair(i, c, None), (cm0, ms))
    carry = lax.cond(q_tile > 0, lambda c: pair(q_tile - 1, c, diag_mask(0)), lambda c: c, carry)
    cm0, ms = carry
    cm1 = scores(2 * q_tile + 1, 1, diag_mask(1))
    ms = consume(2 * q_tile, 0, cm0, ms)
    consume(2 * q_tile + 1, 1, cm1, ms)
    return [(acc_ref[g, :dv, :], acc_ref[g, dv:dv + 1, :]) for g in heads]


def _lambda_full(lq1_ref, lk1_ref, lq2_ref, lk2_ref, lam_init):
    a = jnp.sum(lq1_ref[...] * lk1_ref[...], axis=-1, keepdims=True)
    b = jnp.sum(lq2_ref[...] * lk2_ref[...], axis=-1, keepdims=True)
    return jnp.exp(a) - jnp.exp(b) + lam_init


def _flash_diff_kernel(tq, tk, lam_init, q_ref, k_ref, vt_ref, lq1_ref, lk1_ref, lq2_ref, lk2_ref,
                       gsub_ref, o_ref, s_ref, acc_ref):
    heads = q_ref.shape[2] // DV_A
    qs = []
    for g in range(heads):
        q = q_ref[0, :, g * DV_A:(g + 1) * DV_A]
        lane = lax.broadcasted_iota(jnp.int32, q.shape, 1)
        zero = jnp.zeros_like(q)
        qs.append(jnp.concatenate(
            [jnp.where(lane < DQK, q, zero), jnp.where(lane >= DQK, q, zero)], axis=0))
    res = _flash_cols(qs, k_ref, vt_ref, s_ref, acc_ref, DV_A, DV_A, pl.program_id(2), tq, tk,
                      lambda r: jnp.where(r >= tq, r - tq, r))
    lam = _lambda_full(lq1_ref, lk1_ref, lq2_ref, lk2_ref, lam_init)
    for g, (acc, l) in enumerate(res):
        o = acc / l
        o = o[:, :tq] - lam * o[:, tq:]
        o = o * lax.rsqrt(jnp.mean(o * o, axis=0, keepdims=True) + SUBLN_EPS)
        o_ref[0, :, g * DV_A:(g + 1) * DV_A] = (
            o * gsub_ref[...] * (1.0 - lam_init)).T.astype(o_ref.dtype)


def _flash_mla_kernel(tq, tk, q_ref, k_ref, vt_ref, o_ref, s_ref, acc_ref):
    heads = q_ref.shape[2] // 256
    qs = [q_ref[0, :, g * 256:(g + 1) * 256] for g in range(heads)]
    res = _flash_cols(qs, k_ref, vt_ref, s_ref, acc_ref, 256, V_DIM, pl.program_id(2), tq, tk,
                      lambda r: r)
    for g, (acc, l) in enumerate(res):
        o_ref[0, :, g * V_DIM:(g + 1) * V_DIM] = (acc / l).T.astype(o_ref.dtype)


def _flash_diff(qa, k16, v16t, lam_refs, gsub_col, lam_init, b, s, tk):
    tq = 2 * tk
    heads = FLASH_DIFF_HEADS
    gw = heads * DV_A
    vec = pl.BlockSpec((1, DQK), lambda bi, hi, qi: (0, 0))
    return pl.pallas_call(
        functools.partial(_flash_diff_kernel, tq, tk, lam_init),
        grid=(b, H_A // heads, s // tq),
        in_specs=[
            pl.BlockSpec((1, tq, gw), lambda bi, hi, qi: (bi, qi, hi)),
            pl.BlockSpec((1, s, gw), lambda bi, hi, qi: (bi, 0, hi)),
            pl.BlockSpec((1, s // tk, heads * (DV_A + ONES_ROWS), tk), lambda bi, hi, qi: (bi, 0, hi, 0)),
            vec, vec, vec, vec,
            pl.BlockSpec((DV_A, 1), lambda bi, hi, qi: (0, 0)),
        ],
        out_specs=pl.BlockSpec((1, tq, gw), lambda bi, hi, qi: (bi, qi, hi)),
        out_shape=jax.ShapeDtypeStruct((b, s, W_A), BF16),
        scratch_shapes=[pltpu.VMEM((2, heads, tk, 2 * tq), F32), pltpu.VMEM((heads, DV_A + ONES_ROWS, 2 * tq), F32)],
        compiler_params=_params("arbitrary", "arbitrary", "arbitrary"),
        name="flash_diff",
    )(qa, k16, v16t, *lam_refs, gsub_col)


def _flash_mla(qcat, kcat, vmlat, b, s, tk):
    tq = 2 * tk
    g = FLASH_MLA_HEADS
    return pl.pallas_call(
        functools.partial(_flash_mla_kernel, tq, tk),
        grid=(b, H_B // g, s // tq),
        in_specs=[
            pl.BlockSpec((1, tq, g * 256), lambda bi, hi, qi: (bi, qi, hi)),
            pl.BlockSpec((1, s, g * 256), lambda bi, hi, qi: (bi, 0, hi)),
            pl.BlockSpec((1, s // tk, g * (V_DIM + ONES_ROWS), tk), lambda bi, hi, qi: (bi, 0, hi, 0)),
        ],
        out_specs=pl.BlockSpec((1, tq, g * V_DIM), lambda bi, hi, qi: (bi, qi, hi)),
        out_shape=jax.ShapeDtypeStruct((b, s, W_B), BF16),
        scratch_shapes=[pltpu.VMEM((2, g, tk, tq), F32), pltpu.VMEM((g, V_DIM + ONES_ROWS, tq), F32)],
        compiler_params=_params("arbitrary", "arbitrary", "arbitrary"),
        name="flash_mla",
    )(qcat, kcat, vmlat)


def _dec_update(s, v, m_ref, l_ref, acc_ref, g):
    m_new, l, acc = _online_softmax_step(s, v, m_ref[g], l_ref[g], acc_ref[g])
    m_ref[g] = m_new
    l_ref[g] = l
    acc_ref[g] = acc


def _dec_diff_kernel(n, lam_init, q_ref, k_ref, v_ref, kn_ref, vn_ref, lq1_ref, lk1_ref, lq2_ref,
                     lk2_ref, gsub_ref, o_ref, qbd_ref, m_ref, l_ref, acc_ref):
    j = pl.program_id(1)
    gw = 4 * DV_A

    @pl.when(j == 0)
    def _():
        q = q_ref[...]
        for g in range(2):
            qg = q[:, g * gw:(g + 1) * gw]
            lane = lax.broadcasted_iota(jnp.int32, qg.shape, 1)
            blocks = [jnp.where((lane >> DQK_SHIFT) == blk, qg, jnp.zeros_like(qg))
                      for blk in range(8)]
            qbd_ref[g] = jnp.concatenate(blocks, axis=0)
        m_ref[...] = jnp.full(m_ref.shape, NEG_INF, F32)
        l_ref[...] = jnp.zeros(l_ref.shape, F32)
        acc_ref[...] = jnp.zeros(acc_ref.shape, F32)

    def group_lanes(ref, g):
        keys = ref.shape[0] // H_A
        return jnp.concatenate(
            [ref[pl.ds(g * 4 + hl, keys, stride=H_A), :] for hl in range(4)], axis=1).astype(BF16)

    def attend(kref, vref):
        scores = [_dot_nt(qbd_ref[g], group_lanes(kref, g)) for g in range(2)]
        for g in range(2):
            _dec_update(scores[g], group_lanes(vref, g), m_ref, l_ref, acc_ref, g)

    attend(k_ref.at[0], v_ref.at[0])

    @pl.when(j == pl.num_programs(1) - 1)
    def _():
        lam = _lambda_full(lq1_ref, lk1_ref, lq2_ref, lk2_ref, lam_init)
        attend(kn_ref, vn_ref)
        for g in range(2):
            o = acc_ref[g] / l_ref[g]
            for hl in range(4):
                cols = slice(hl * DV_A, (hl + 1) * DV_A)
                o0 = o[(2 * hl) * n:(2 * hl + 1) * n, cols]
                o1 = o[(2 * hl + 1) * n:(2 * hl + 2) * n, cols]
                od = _rms(o0 - lam * o1, SUBLN_EPS) * gsub_ref[...] * (1.0 - lam_init)
                hd = g * 4 + hl
                o_ref[:, hd * DV_A:(hd + 1) * DV_A] = od.astype(o_ref.dtype)


def _dec_diff(qa, cache_k, cache_v, k_new, v_new, lam_refs, gsub, lam_init, b, n):
    past = cache_k.shape[1] // H_A
    tk = 1024
    vec = pl.BlockSpec((1, DQK), lambda bi, j: (0, 0))
    rows = pl.BlockSpec((n, W_A), lambda bi, j: (bi, 0))
    new_rows = pl.BlockSpec((n * H_A, DV_A), lambda bi, j: (bi, 0))
    cache_tile = pl.BlockSpec((1, tk * H_A, DV_A), lambda bi, j: (bi, j, 0))
    return pl.pallas_call(
        functools.partial(_dec_diff_kernel, n, lam_init),
        grid=(b, past // tk),
        in_specs=[
            rows, cache_tile, cache_tile, new_rows, new_rows, vec, vec, vec, vec,
            pl.BlockSpec((1, DV_A), lambda bi, j: (0, 0)),
        ],
        out_specs=rows,
        out_shape=jax.ShapeDtypeStruct((b * n, W_A), BF16),
        scratch_shapes=[
            pltpu.VMEM((2, 8 * n, 4 * DV_A), BF16),
            pltpu.VMEM((2, 8 * n, 1), F32),
            pltpu.VMEM((2, 8 * n, 1), F32),
            pltpu.VMEM((2, 8 * n, 4 * DV_A), F32),
        ],
        compiler_params=_params("arbitrary", "arbitrary"),
        name="dec_diff",
    )(qa, cache_k, cache_v, k_new, v_new, *lam_refs, gsub)


def _dec_mla_kernel(n, qlat_ref, qpe_ref, ckv_ref, kpet_ref, ckvn_ref, kpen_ref, wuv_ref, o_ref,
                    ql_ref, qp_ref, m_ref, l_ref, acc_ref):
    j = pl.program_id(1)

    @pl.when(j == 0)
    def _():
        for hd in range(H_B):
            ql_ref[hd * n:(hd + 1) * n, :] = qlat_ref[:, hd * KV_LORA:(hd + 1) * KV_LORA]
            qp_ref[hd * n:(hd + 1) * n, :] = qpe_ref[:, hd * 128:(hd + 1) * 128]
        m_ref[...] = jnp.full(m_ref.shape, NEG_INF, F32)
        l_ref[...] = jnp.zeros(l_ref.shape, F32)
        acc_ref[...] = jnp.zeros(acc_ref.shape, F32)

    qp = qp_ref[:, :ROPE_DIM]
    half = ckv_ref.shape[1] // 2
    halves = [slice(i * half, (i + 1) * half) for i in range(2)]
    ckvs = [ckv_ref[0, sl, :].astype(BF16) for sl in halves]
    scores = [_dot_nt(ql_ref[...], ckv) + _dot(qp, kpet_ref[0, :, sl].astype(BF16))
              for ckv, sl in zip(ckvs, halves)]
    for s, ckv in zip(scores, ckvs):
        _dec_update(s, ckv, m_ref, l_ref, acc_ref, 0)

    @pl.when(j == pl.num_programs(1) - 1)
    def _():
        ckv = ckvn_ref[...].astype(BF16)
        s = _dot_nt(ql_ref[...], ckv) + _dot_nt(qp, kpen_ref[...].astype(BF16))
        _dec_update(s, ckv, m_ref, l_ref, acc_ref, 0)
        o = (acc_ref[0] / l_ref[0]).astype(BF16)
        for hd in range(H_B):
            o_ref[:, hd * V_DIM:(hd + 1) * V_DIM] = _dot(
                o[hd * n:(hd + 1) * n], wuv_ref[hd]).astype(o_ref.dtype)


def _dec_mla(qlat, qpe, cache_ckv, cache_kpe_t, ckv_new, kpe_new, wuv_h, b, n):
    past = cache_ckv.shape[1]
    tk = 1024
    return pl.pallas_call(
        functools.partial(_dec_mla_kernel, n),
        grid=(b, past // tk),
        in_specs=[
            pl.BlockSpec((n, H_B * KV_LORA), lambda bi, j: (bi, 0)),
            pl.BlockSpec((n, H_B * 128), lambda bi, j: (bi, 0)),
            pl.BlockSpec((1, tk, KV_LORA), lambda bi, j: (bi, j, 0)),
            pl.BlockSpec((1, ROPE_DIM, tk), lambda bi, j: (bi, 0, j)),
            pl.BlockSpec((n, KV_LORA), lambda bi, j: (bi, 0)),
            pl.BlockSpec((n, ROPE_DIM), lambda bi, j: (bi, 0)),
            pl.BlockSpec((H_B, KV_LORA, V_DIM), lambda bi, j: (0, 0, 0)),
        ],
        out_specs=pl.BlockSpec((n, W_B), lambda bi, j: (bi, 0)),
        out_shape=jax.ShapeDtypeStruct((b * n, W_B), BF16),
        scratch_shapes=[
            pltpu.VMEM((H_B * n, KV_LORA), BF16),
            pltpu.VMEM((H_B * n, 128), BF16),
            pltpu.VMEM((1, H_B * n, 1), F32),
            pltpu.VMEM((1, H_B * n, 1), F32),
            pltpu.VMEM((1, H_B * n, KV_LORA), F32),
        ],
        compiler_params=_params("arbitrary", "arbitrary"),
        name="dec_mla",
    )(qlat, qpe, cache_ckv, cache_kpe_t, ckv_new, kpe_new, wuv_h)


def _out_proj_kernel(oa_ref, ob_ref, x_ref, shift_ref, scale_ref, gate_ref, gn_ref, wgate_ref, w_ref,
                     o_ref):
    nb = x_ref.shape[0]
    for _, rs, ts in _token_chains(x_ref):
        sg = _silu(_dot_nt(_prenorm_tokens(x_ref, rs, gn_ref, scale_ref, shift_ref), wgate_ref[...]))
        oa = (oa_ref[ts, :].astype(F32) * sg[:, :W_A]).astype(BF16)
        ob = (ob_ref[ts, :].astype(F32) * sg[:, W_A:]).astype(BF16)
        y = _dot(oa, w_ref[:W_A, :]) + _dot(ob, w_ref[W_A:, :])
        o_ref[:, rs, :] = x_ref[:, rs, :] + gate_ref[...] * y.reshape(nb, -1, D_MODEL)


def _pool_kernel(tiles_per_seq, pos0, carry, x_ref, shift_ref, scale_ref, gate_ref, gn_ref, pre_ref,
                 win_ref, wgrp_ref, sp_ref, wout_ref, gf_ref, y_ref, st_ref, uext_ref):
    nb, rows, _ = x_ref.shape
    t = pl.program_id(0) % tiles_per_seq
    if carry:
        @pl.when(t == 0)
        def _():
            uext_ref[:, :POOL_PREFIX, :] = pre_ref[...]
    else:
        uext_ref[:, :POOL_PREFIX, :] = pre_ref[...]

    for r0, r1 in _row_chains(rows):
        n = r1 - r0
        x = x_ref[:, r0:r1, :]
        h = _prenorm(x, gn_ref[...], scale_ref[...], shift_ref[...])
        z = _dot(h.reshape(nb * n, D_MODEL).astype(BF16), win_ref[...])
        u = z[:, :W_C].reshape(nb, n, W_C)
        sgate = _silu(z[:, W_C:])
        uext_ref[:, POOL_PREFIX + r0:POOL_PREFIX + r1, :] = u
        pos = pos0 + t * rows + r0 + lax.broadcasted_iota(jnp.int32, (1, n, 1), 1)
        parts = []
        for g, w in enumerate(POOL_WINDOWS):
            cols = slice(g * POOL_GW, (g + 1) * POOL_GW)
            ws = u[:, :, cols]
            for k in range(1, w):
                ws = ws + uext_ref[:, POOL_PREFIX + r0 - k:POOL_PREFIX + r1 - k, cols]
            cnt = jnp.minimum(pos + 1, w).astype(F32)
            dgrp = (ws / cnt - u[:, :, cols]).reshape(nb * n, POOL_GW).astype(BF16)
            parts.append(_dot(dgrp, wgrp_ref[g]))
        m = jnp.concatenate(parts, axis=1) * sp_ref[...]
        y = _dot((m * sgate).astype(BF16), wout_ref[...])
        x2 = x + gate_ref[...] * y.reshape(nb, n, D_MODEL)
        y_ref[:, r0:r1, :] = _rms(x2, EPS) * gf_ref[...]
    tail = uext_ref[:, rows:rows + POOL_PREFIX, :]
    st_ref[...] = tail
    if carry:
        uext_ref[:, :POOL_PREFIX, :] = tail


def _rot_half_cols(w):
    half = ROPE_DIM // 2
    return jnp.concatenate([-w[..., half:], w[..., :half]], axis=-1)


def _rope_tables(pos):
    half = ROPE_DIM // 2
    inv = ROPE_THETA ** (-jnp.arange(half, dtype=F32) / half)
    ang = pos.astype(F32)[:, None] * inv[None, :]
    cos, sin = jnp.cos(ang), jnp.sin(ang)
    zero = jnp.zeros((pos.shape[0], 128 - ROPE_DIM), F32)
    return jnp.concatenate([cos, cos, zero], axis=1), jnp.concatenate([sin, sin, zero], axis=1)


def _group(x3, mod, cos, sin, sample, cache, w, tm_att, tm_pool):
    bt, st, _ = x3.shape
    tokens = bt * st
    lam_init = _lambda_init(0)

    if sample:
        nb, rows, tps = SAMPLE_TILE_SEQS, st, 1
        n_tiles = bt // nb
        x_spec = pl.BlockSpec((nb, rows, D_MODEL), lambda i: (i, 0, 0))

        def mod_spec(layer, comp):
            blk = (layer * 3 + comp) * MOD_ROWS // nb
            return pl.BlockSpec((nb, 1, D_MODEL), lambda i: (blk + i, 0, 0))
    else:
        nb, rows = 1, tm_att
        tps = st // rows
        n_tiles = bt * tps
        x_spec = pl.BlockSpec((1, rows, D_MODEL), lambda i: (i // tps, i % tps, 0))

        def mod_spec(layer, comp):
            base = (layer * 3 + comp) * MOD_ROWS + 16
            return pl.BlockSpec((1, 1, D_MODEL), lambda i: (base + i // tps, 0, 0))

    tm = nb * rows

    def tok(width):
        return pl.BlockSpec((tm, width), lambda i: (i, 0))

    def tok_shape(width, dt):
        return jax.ShapeDtypeStruct((tokens, width), dt)

    vec = lambda width: pl.BlockSpec((1, width), lambda i: (0, 0))
    if sample:
        tab = pl.BlockSpec((tm, 128), lambda i: (0, 0))
    else:
        tab = pl.BlockSpec((tm, 128), lambda i: (i % tps, 0))

    heads_spec = pl.BlockSpec((tm * H_A, DV_A), lambda i: (i, 0))
    heads_shape = jax.ShapeDtypeStruct((tokens * H_A, DV_A), F32)
    vt_rows = H_A * (DV_A + ONES_ROWS)
    key_tile = min(tm, CHAIN_ROWS)
    vt_spec = pl.BlockSpec((1, tm // key_tile, vt_rows, key_tile), lambda i: (i // tps, i % tps, 0, 0))
    vt_shape = jax.ShapeDtypeStruct((bt, st // key_tile, vt_rows, key_tile), BF16)
    qkv_out = pl.pallas_call(
        _in_qkv_kernel,
        grid=(n_tiles,),
        in_specs=[x_spec, mod_spec(0, 0), mod_spec(0, 1), vec(D_MODEL), _resident((3 * W_A, D_MODEL))],
        out_specs=[tok(W_A), heads_spec, heads_spec] + ([] if sample else [tok(W_A), vt_spec]),
        out_shape=[tok_shape(W_A, BF16), heads_shape, heads_shape]
        + ([] if sample else [tok_shape(W_A, BF16), vt_shape]),
        compiler_params=_params("arbitrary"),
        name="in_qkv",
    )(x3, mod, mod, w["g_norm0"], w["wqkv"])
    qa, k32, v32 = qkv_out[:3]

    lat_in = [x_spec, mod_spec(0, 0), mod_spec(0, 1), vec(D_MODEL), tab, tab,
              _resident((Q_LORA + KV_LORA, D_MODEL)), _resident((256, D_MODEL)),
              vec(Q_LORA), vec(KV_LORA),
              _resident((Q_LORA, H_B * 128)), _resident((Q_LORA, H_B * 128)), _resident((Q_LORA, H_B * 128))]
    lat_args = [x3, mod, mod, w["g_norm0"], cos, sin, w["wc"], w["wkpe"], w["g_qa"], w["g_kva"],
                w["wuq"], w["wuqr"], w["wuqrr"]]
    lam_refs = (w["lam_q1"], w["lam_k1"], w["lam_q2"], w["lam_k2"])

    if sample:
        ckv32, kpe32, qlat, qpe = pl.pallas_call(
            _in_lat_sample_kernel,
            grid=(n_tiles,),
            in_specs=lat_in + [_resident((H_B, 128, KV_LORA))],
            out_specs=[tok(KV_LORA), tok(ROPE_DIM), tok(H_B * KV_LORA), tok(H_B * 128)],
            out_shape=[tok_shape(KV_LORA, F32), tok_shape(ROPE_DIM, F32),
                       tok_shape(H_B * KV_LORA, BF16), tok_shape(H_B * 128, BF16)],
            compiler_params=_params("arbitrary"),
            name="in_lat_sample",
        )(*lat_args, w["wukt"])
        cache_k, cache_v, cache_ckv, cache_kpe = cache
        o_a = _dec_diff(qa, cache_k, cache_v, k32, v32, lam_refs, w["g_subln"], lam_init, bt, st)
        o_b = _dec_mla(qlat, qpe, cache_ckv, cache_kpe, ckv32, kpe32, w["wuv_h"], bt, st)
    else:
        ckv32, kpe32, qcat, kcat, vmla = pl.pallas_call(
            _in_lat_prompt_kernel,
            grid=(n_tiles,),
            in_specs=lat_in + [_resident((KV_LORA, H_B * 128)), _resident((KV_LORA, H_B * 128))],
            out_specs=[tok(KV_LORA), tok(ROPE_DIM), tok(H_B * 256), tok(H_B * 256), vt_spec],
            out_shape=[tok_shape(KV_LORA, F32), tok_shape(ROPE_DIM, F32),
                       tok_shape(H_B * 256, BF16), tok_shape(H_B * 256, BF16), vt_shape],
            compiler_params=_params("arbitrary"),
            name="in_lat_prompt",
        )(*lat_args, w["wuk"], w["wuv"])
        k16, v16t = qkv_out[3:]
        o_a = _flash_diff(qa.reshape(bt, st, W_A), k16.reshape(bt, st, W_A), v16t, lam_refs,
                          w["g_subln"].reshape(DV_A, 1), lam_init, bt, st, key_tile).reshape(tokens, W_A)
        o_b = _flash_mla(qcat.reshape(bt, st, H_B * 256), kcat.reshape(bt, st, H_B * 256),
                         vmla, bt, st, key_tile).reshape(tokens, W_B)

    x1 = pl.pallas_call(
        _out_proj_kernel,
        grid=(n_tiles,),
        in_specs=[tok(W_A), tok(W_B), x_spec, mod_spec(0, 0), mod_spec(0, 1), mod_spec(0, 2), vec(D_MODEL),
                  _resident((W_ATT, D_MODEL)), _resident((W_ATT, D_MODEL))],
        out_specs=x_spec,
        out_shape=jax.ShapeDtypeStruct(x3.shape, F32),
        compiler_params=_params("arbitrary"),
        name="out_proj",
    )(o_a, o_b, x3, mod, mod, mod, w["g_norm0"], w["wgate"], w["wout_att"])

    if sample:
        p_nb, p_rows, p_tiles, p_tps = nb, st, n_tiles, 1
        px_spec = x_spec
        pmod = mod_spec
        pre_spec = pl.BlockSpec((nb, POOL_PREFIX, W_C), lambda i: (i, 0, 0))
        st_spec = pl.BlockSpec((nb, POOL_PREFIX, W_C), lambda i: (i, 0, 0))
        pos0 = cache[2].shape[1]
        prefix = w["pool_prefix"]
    else:
        p_nb, p_rows = 1, tm_pool
        p_tps = st // p_rows
        p_tiles = bt * p_tps
        px_spec = pl.BlockSpec((1, p_rows, D_MODEL), lambda i: (i // p_tps, i % p_tps, 0))

        def pmod(layer, comp):
            base = (layer * 3 + comp) * MOD_ROWS + 16
            return pl.BlockSpec((1, 1, D_MODEL), lambda i: (base + i // p_tps, 0, 0))

        pre_spec = pl.BlockSpec((1, POOL_PREFIX, W_C), lambda i: (0, 0, 0))
        st_spec = pl.BlockSpec((1, POOL_PREFIX, W_C), lambda i: (i // p_tps, 0, 0))
        pos0 = 0
        prefix = jnp.zeros((1, POOL_PREFIX, W_C), F32)

    y, state = pl.pallas_call(
        functools.partial(_pool_kernel, p_tps, pos0, not sample),
        grid=(p_tiles,),
        in_specs=[px_spec, pmod(1, 0), pmod(1, 1), pmod(1, 2), vec(D_MODEL), pre_spec,
                  _resident((D_MODEL, 2 * W_C)), _resident((N_POOL_GROUPS, POOL_GW, POOL_GW)),
                  vec(W_C), _resident((W_C, D_MODEL)), vec(D_MODEL)],
        out_specs=[px_spec, st_spec],
        out_shape=[jax.ShapeDtypeStruct(x3.shape, F32),
                   jax.ShapeDtypeStruct((bt, POOL_PREFIX, W_C), F32)],
        scratch_shapes=[pltpu.VMEM((p_nb, POOL_PREFIX + p_rows, W_C), F32)],
        compiler_params=_params("arbitrary"),
        name="pool",
    )(x1, mod, mod, mod, w["g_norm1"], prefix, w["win_pool"], w["wgrp"], w["scale_pool"],
      w["wout_pool"], w["g_final"])

    return (y,
            k32.reshape(1, bt, st, H_A, DV_A), v32.reshape(1, bt, st, H_A, DV_A),
            ckv32.reshape(1, bt, st, KV_LORA), kpe32.reshape(1, bt, st, ROPE_DIM),
            state[None, :, 1:, :])


def kernel(x_prompt, x_sample, c_prompt, c_sample, cache_diff_k, cache_diff_v, cache_mla_ckv, cache_mla_kpe, state_pool, g_norm, w_ada, b_ada, w_in_att, w_out_att, lam_q1, lam_k1, lam_q2, lam_k2, g_subln, g_qa, w_uq, w_uqr, w_uk, w_uv, g_kva, w_in_pool, w_grp_pool, scale_pool, w_out_pool, g_final):
    b_p, s_p, _ = x_prompt.shape
    b_s, n_s, _ = x_sample.shape
    past = cache_diff_k.shape[2]

    c_all = jnp.concatenate(
        [c_sample, c_prompt, jnp.zeros((MOD_ROWS - b_s - b_p, D_MODEL), F32)], axis=0)
    mod = _ada(c_all, w_ada, b_ada).reshape(-1, 1, D_MODEL)

    w_in_t = w_in_att[0].T
    kpe_w = w_in_t[4 * W_A:4 * W_A + ROPE_DIM].T
    zpad = jnp.zeros((D_MODEL, 128 - ROPE_DIM), F32)
    uqr = w_uqr[0]
    pad_heads = lambda a: jnp.pad(a, ((0, 0), (0, 0), (0, 128 - ROPE_DIM))).reshape(Q_LORA, H_B * 128)
    w = {
        "g_norm0": g_norm[0:1], "g_norm1": g_norm[1:2],
        "wqkv": w_in_t[:3 * W_A].astype(BF16),
        "wc": w_in_t[3 * W_A:4 * W_A].astype(BF16),
        "wkpe": jnp.concatenate([kpe_w, zpad, _rot_half_cols(kpe_w), zpad], axis=1).T.astype(BF16),
        "wgate": w_in_t[4 * W_A + ROPE_DIM:].astype(BF16),
        "g_qa": g_qa, "g_kva": g_kva,
        "wuq": w_uq[0].reshape(Q_LORA, H_B * NOPE_DIM).astype(BF16),
        "wuqr": pad_heads(uqr).astype(BF16),
        "wuqrr": pad_heads(_rot_half_cols(uqr)).astype(BF16),
        "wuk": w_uk[0].reshape(KV_LORA, H_B * NOPE_DIM).astype(BF16),
        "wuv": w_uv[0].reshape(KV_LORA, H_B * V_DIM).astype(BF16),
        "wukt": jnp.transpose(w_uk[0], (1, 2, 0)).astype(BF16),
        "wuv_h": jnp.transpose(w_uv[0], (1, 0, 2)).astype(BF16),
        "wout_att": w_out_att[0].astype(BF16),
        "lam_q1": lam_q1, "lam_k1": lam_k1, "lam_q2": lam_q2, "lam_k2": lam_k2,
        "g_subln": g_subln,
        "win_pool": w_in_pool[0].astype(BF16),
        "wgrp": w_grp_pool[0].astype(BF16),
        "scale_pool": scale_pool,
        "wout_pool": w_out_pool[0].astype(BF16),
        "g_final": g_final.reshape(1, D_MODEL),
        "pool_prefix": jnp.pad(state_pool[0], ((0, 0), (1, 0), (0, 0))),
    }

    cos_p, sin_p = _rope_tables(jnp.arange(s_p, dtype=jnp.int32))
    cos_s, sin_s = _rope_tables(past + jnp.arange(n_s, dtype=jnp.int32))
    cos_s = jnp.tile(cos_s, (SAMPLE_TILE_SEQS, 1))
    sin_s = jnp.tile(sin_s, (SAMPLE_TILE_SEQS, 1))

    out_p = _group(x_prompt, mod, cos_p, sin_p, False, None, w, 2 * CHAIN_ROWS, 2 * CHAIN_ROWS)
    cache = (cache_diff_k[0].reshape(b_s, past * H_A, DV_A), cache_diff_v[0].reshape(b_s, past * H_A, DV_A),
             cache_mla_ckv[0], jnp.swapaxes(cache_mla_kpe[0], 1, 2))
    out_s = _group(x_sample, mod, cos_s, sin_s, True, cache, w, 0, 0)
    return (out_p[0], out_s[0]) + out_p[1:] + out_s[1:]
```

```python
import functools
import math

import jax
import jax.numpy as jnp
import numpy as np
from jax import lax
from jax.experimental import pallas as pl
from jax.experimental.pallas import tpu as pltpu

F32 = jnp.float32
BF16 = jnp.bfloat16

D_MODEL = 2048
CHUNK = 64
CHUNK_SHIFT = 6
H_A = 8
DQK = 64
DQK_SHIFT = 6
DV_A = 2 * DQK
W_A = H_A * DV_A
H_B = 8
Q_LORA = 512
KV_LORA = 512
NOPE_DIM = 128
ROPE_DIM = 64
V_DIM = 128
W_B = H_B * V_DIM
ROPE_THETA = 10000.0
W_ATT = W_A + W_B
POOL_WINDOWS = (2, 4, 8, 16)
N_POOL_GROUPS = 4
W_C = D_MODEL
POOL_GW = W_C // N_POOL_GROUPS
POOL_STATE = max(POOL_WINDOWS) - 1
POOL_PREFIX = POOL_STATE + 1
EPS = 1e-6
SUBLN_EPS = 1e-5
NEG_INF = -1e30
LOG2E = math.log2(math.e)
DIFF_QSCALE = DQK ** -0.5 * LOG2E
MLA_QSCALE = (NOPE_DIM + ROPE_DIM) ** -0.5 * LOG2E
MOD_ROWS = 32
FLASH_DIFF_HEADS = 2
FLASH_MLA_HEADS = 4
DEC_SEQS = 2
DEC_MLA_KEYS = 2048
DEC_DIFF_SEQS = 1
DEC_DIFF_KEYS = 2048
CHAIN_ROWS = 256
ONES_ROWS = 16
SAMPLE_TILE_SEQS = 8
VMEM_LIMIT = 56 * 1024 * 1024


def _lambda_init(layer_idx):
    return 0.8 - 0.6 * math.exp(-0.3 * layer_idx)


def _rms(x, eps):
    return x * lax.rsqrt(jnp.mean(x * x, axis=-1, keepdims=True) + eps)


def _silu(x):
    return x * (1.0 / (1.0 + jnp.exp(-x)))


def _dot(a, b):
    return jnp.dot(a, b, preferred_element_type=F32)


def _dot_nt(a, b):
    return lax.dot_general(a, b, (((1,), (1,)), ((), ())), preferred_element_type=F32)


def _resident(shape):
    nd = len(shape)
    return pl.BlockSpec(shape, lambda *_: (0,) * nd, pipeline_mode=pl.Buffered(1))


def _params(*sem):
    return pltpu.CompilerParams(dimension_semantics=sem, vmem_limit_bytes=VMEM_LIMIT)


def _ada_kernel(c_ref, w_ref, b_ref, o_ref):
    cond = _silu(c_ref[...]).astype(BF16)
    w = w_ref[0].astype(BF16)
    o_ref[0, 0] = _dot(cond, w) + b_ref[0, 0]


def _ada(c_all, w_ada, b_ada):
    depth = w_ada.shape[0]
    tn = 1024
    per = D_MODEL // tn
    return pl.pallas_call(
        _ada_kernel,
        grid=(depth, 3 * per),
        in_specs=[
            pl.BlockSpec((MOD_ROWS, D_MODEL), lambda l, j: (0, 0)),
            pl.BlockSpec((1, D_MODEL, tn), lambda l, j: (l, 0, j)),
            pl.BlockSpec((1, 1, 1, tn), lambda l, j: (l, j // per, 0, j % per)),
        ],
        out_specs=pl.BlockSpec((1, 1, MOD_ROWS, tn), lambda l, j: (l, j // per, 0, j % per)),
        out_shape=jax.ShapeDtypeStruct((depth, 3, MOD_ROWS, D_MODEL), F32),
        compiler_params=_params("arbitrary", "arbitrary"),
        name="ada",
    )(c_all, w_ada, b_ada.reshape(depth, 3, 1, D_MODEL))


def _row_chains(rows):
    step = min(rows, CHAIN_ROWS)
    return [(r, r + step) for r in range(0, rows, step)]


def _prenorm(x, gn, scale, shift):
    return _rms(x, EPS) * gn * (1.0 + scale) + shift


def _store_vt(vt_ref, tile, v, heads, dv):
    vt = v.T.astype(BF16)
    ones = jnp.ones((ONES_ROWS, v.shape[0]), BF16)
    for hd in range(heads):
        base = hd * (dv + ONES_ROWS)
        vt_ref[0, tile, base:base + dv, :] = vt[hd * dv:(hd + 1) * dv]
        vt_ref[0, tile, base + dv:base + dv + ONES_ROWS, :] = ones


def _token_chains(x_ref):
    nb, rows, _ = x_ref.shape
    chains = _row_chains(rows)
    assert nb == 1 or len(chains) == 1
    return [(c, slice(r0, r1), slice(nb * r0, nb * r1)) for c, (r0, r1) in enumerate(chains)]


def _prenorm_tokens(x_ref, rs, gn_ref, scale_ref, shift_ref):
    x = x_ref[:, rs, :]
    h = _prenorm(x, gn_ref[...], scale_ref[...], shift_ref[...])
    return h.reshape(x.shape[0] * x.shape[1], D_MODEL).astype(BF16)


def _in_qkv_kernel(x_ref, shift_ref, scale_ref, gn_ref, wqkv_ref, qa_ref, k32_ref, v32_ref, *flash_refs):
    for c, rs, ts in _token_chains(x_ref):
        z = _dot_nt(_prenorm_tokens(x_ref, rs, gn_ref, scale_ref, shift_ref), wqkv_ref[...])
        qa_ref[ts, :] = (z[:, :W_A] * DIFF_QSCALE).astype(BF16)
        ka = z[:, W_A:2 * W_A]
        va = z[:, 2 * W_A:]
        for hd in range(H_A):
            head_rows = pl.ds(ts.start * H_A + hd, ts.stop - ts.start, stride=H_A)
            k32_ref[head_rows, :] = ka[:, hd * DV_A:(hd + 1) * DV_A]
            v32_ref[head_rows, :] = va[:, hd * DV_A:(hd + 1) * DV_A]
        if flash_refs:
            k16_ref, vt_ref = flash_refs
            k16_ref[ts, :] = ka.astype(BF16)
            _store_vt(vt_ref, c, va, H_A, DV_A)


def _latent_chain(x_ref, rs, ts, shift_ref, scale_ref, gn_ref, cos_ref, sin_ref, wc_ref, wkpe_ref,
                  gqa_ref, gkva_ref, wuq_ref, wuqr_ref, wuqrr_ref, ckv32_ref, kpe32_ref):
    hb = _prenorm_tokens(x_ref, rs, gn_ref, scale_ref, shift_ref)
    zc = _dot_nt(hb, wc_ref[...])
    cq = (_rms(zc[:, :Q_LORA], EPS) * gqa_ref[...]).astype(BF16)
    ckv = _rms(zc[:, Q_LORA:], EPS) * gkva_ref[...]
    ckv32_ref[ts, :] = ckv
    cos = cos_ref[ts, :]
    sin = sin_ref[ts, :]
    zk = _dot_nt(hb, wkpe_ref[...])
    kpe_pad = zk[:, :128] * cos + zk[:, 128:] * sin
    kpe32_ref[ts, :] = kpe_pad[:, :ROPE_DIM]
    q_nope = _dot(cq, wuq_ref[...]) * MLA_QSCALE
    cos8 = jnp.tile(cos, (1, H_B))
    sin8 = jnp.tile(sin, (1, H_B))
    q_pe = (_dot(cq, wuqr_ref[...]) * cos8 + _dot(cq, wuqrr_ref[...]) * sin8) * MLA_QSCALE
    return ckv, kpe_pad, q_nope, q_pe


def _in_lat_prompt_kernel(x_ref, shift_ref, scale_ref, gn_ref, cos_ref, sin_ref, wc_ref, wkpe_ref,
                          gqa_ref, gkva_ref, wuq_ref, wuqr_ref, wuqrr_ref, wuk_ref, wuv_ref,
                          ckv32_ref, kpe32_ref, qcat_ref, kcat_ref, vmla_ref):
    for c, rs, ts in _token_chains(x_ref):
        ckv, kpe_pad, q_nope, q_pe = _latent_chain(
            x_ref, rs, ts, shift_ref, scale_ref, gn_ref, cos_ref, sin_ref, wc_ref, wkpe_ref,
            gqa_ref, gkva_ref, wuq_ref, wuqr_ref, wuqrr_ref, ckv32_ref, kpe32_ref)
        ckvb = ckv.astype(BF16)
        k_nope = _dot(ckvb, wuk_ref[...])
        _store_vt(vmla_ref, c, _dot(ckvb, wuv_ref[...]), H_B, V_DIM)
        kpe_b = kpe_pad.astype(BF16)
        for hd in range(H_B):
            sl = slice(hd * 128, (hd + 1) * 128)
            qcat_ref[ts, hd * 256:hd * 256 + 128] = q_nope[:, sl].astype(BF16)
            qcat_ref[ts, hd * 256 + 128:(hd + 1) * 256] = q_pe[:, sl].astype(BF16)
            kcat_ref[ts, hd * 256:hd * 256 + 128] = k_nope[:, sl].astype(BF16)
            kcat_ref[ts, hd * 256 + 128:(hd + 1) * 256] = kpe_b


def _in_lat_sample_kernel(x_ref, shift_ref, scale_ref, gn_ref, cos_ref, sin_ref, wc_ref, wkpe_ref,
                          gqa_ref, gkva_ref, wuq_ref, wuqr_ref, wuqrr_ref, wukt_ref,
                          ckv32_ref, kpe32_ref, qlat_ref, qpe_ref):
    for _, rs, ts in _token_chains(x_ref):
        _, _, q_nope, q_pe = _latent_chain(
            x_ref, rs, ts, shift_ref, scale_ref, gn_ref, cos_ref, sin_ref, wc_ref, wkpe_ref,
            gqa_ref, gkva_ref, wuq_ref, wuqr_ref, wuqrr_ref, ckv32_ref, kpe32_ref)
        qn = q_nope.astype(BF16)
        for hd in range(H_B):
            qlat_ref[ts, hd * KV_LORA:(hd + 1) * KV_LORA] = _dot(
                qn[:, hd * 128:(hd + 1) * 128], wukt_ref[hd]).astype(BF16)
        qpe_ref[ts, :] = q_pe.astype(BF16)


def _online_softmax_step(s, v, m, l, acc):
    m_new = jnp.maximum(m, jnp.max(s, axis=1, keepdims=True))
    alpha = jnp.exp2(m - m_new)
    p = jnp.exp2(s - m_new)
    l = alpha * l + jnp.sum(p, axis=1, keepdims=True)
    acc = alpha * acc + _dot(p.astype(BF16), v)
    return m_new, l, acc


def _flash_cols(qs, k_ref, vt_ref, s_ref, acc_ref, dk, dv, q_tile, tq, tk, pos_of_col):
    cols = qs[0].shape[0]
    dve = dv + ONES_ROWS
    assert tq == 2 * tk
    heads = range(len(qs))
    q_chunk = pos_of_col(lax.broadcasted_iota(jnp.int32, (tk, cols), 1)) >> CHUNK_SHIFT
    k_row = lax.broadcasted_iota(jnp.int32, (tk, cols), 0)

    def diag_mask(d):
        return ((k_row + d * tk) >> CHUNK_SHIFT) <= q_chunk

    def scores(tile, slot, mask):
        off = pl.multiple_of(tile * tk, tk)
        col_max = []
        for g in heads:
            s = _dot_nt(k_ref[0, pl.ds(off, tk), g * dk:(g + 1) * dk], qs[g])
            if mask is not None:
                s = jnp.where(mask, s, NEG_INF)
            s_ref[slot, g] = s
            col_max.append(jnp.max(s, axis=0, keepdims=True))
        return tuple(col_max)

    def consume(tile, slot, col_max, ms):
        out = []
        for g in heads:
            m_new = jnp.maximum(ms[g], col_max[g])
            alpha = jnp.exp2(ms[g] - m_new)
            p = jnp.exp2(s_ref[slot, g] - m_new).astype(BF16)
            acc_ref[g] = alpha * acc_ref[g] + _dot(vt_ref[0, tile, g * dve:(g + 1) * dve, :], p)
            out.append(m_new)
        return tuple(out)

    def pair(i, carry, next_mask):
        cm0, ms = carry
        cm1 = scores(2 * i + 1, 1, None)
        ms = consume(2 * i, 0, cm0, ms)
        cm0 = scores(2 * i + 2, 0, next_mask)
        ms = consume(2 * i + 1, 1, cm1, ms)
        return cm0, ms

    acc_ref[...] = jnp.zeros(acc_ref.shape, F32)
    ms = (jnp.full((1, cols), NEG_INF, F32),) * len(qs)
    cm0 = scores(0, 0, jnp.logical_or(q_tile > 0, diag_mask(0)))
    carry = lax.fori_loop(0, jnp.maximum(q_tile - 1, 0), lambda i, c: pair(i, c, None), (cm0, ms))
    carry = lax.cond(q_tile > 0, lambda c: pair(q_tile - 1, c, diag_mask(0)), lambda c: c, carry)
    cm0, ms = carry
    cm1 = scores(2 * q_tile + 1, 1, diag_mask(1))
    ms = consume(2 * q_tile, 0, cm0, ms)
    consume(2 * q_tile + 1, 1, cm1, ms)
    return [(acc_ref[g, :dv, :], acc_ref[g, dv:dv + 1, :]) for g in heads]


def _lambda_full(lq1_ref, lk1_ref, lq2_ref, lk2_ref, lam_init):
    a = jnp.sum(lq1_ref[...] * lk1_ref[...], axis=-1, keepdims=True)
    b = jnp.sum(lq2_ref[...] * lk2_ref[...], axis=-1, keepdims=True)
    return jnp.exp(a) - jnp.exp(b) + lam_init


def _flash_diff_kernel(tq, tk, lam_init, q_ref, k_ref, vt_ref, lq1_ref, lk1_ref, lq2_ref, lk2_ref,
                       gsub_ref, o_ref, s_ref, acc_ref):
    heads = q_ref.shape[2] // DV_A
    qs = []
    for g in range(heads):
        q = q_ref[0, :, g * DV_A:(g + 1) * DV_A]
        lane = lax.broadcasted_iota(jnp.int32, q.shape, 1)
        zero = jnp.zeros_like(q)
        qs.append(jnp.concatenate(
            [jnp.where(lane < DQK, q, zero), jnp.where(lane >= DQK, q, zero)], axis=0))
    res = _flash_cols(qs, k_ref, vt_ref, s_ref, acc_ref, DV_A, DV_A, pl.program_id(2), tq, tk,
                      lambda r: jnp.where(r >= tq, r - tq, r))
    lam = _lambda_full(lq1_ref, lk1_ref, lq2_ref, lk2_ref, lam_init)
    for g, (acc, l) in enumerate(res):
        o = acc / l
        o = o[:, :tq] - lam * o[:, tq:]
        o = o * lax.rsqrt(jnp.mean(o * o, axis=0, keepdims=True) + SUBLN_EPS)
        o_ref[0, :, g * DV_A:(g + 1) * DV_A] = (
            o * gsub_ref[...] * (1.0 - lam_init)).T.astype(o_ref.dtype)


def _flash_mla_kernel(tq, tk, q_ref, k_ref, vt_ref, o_ref, s_ref, acc_ref):
    heads = q_ref.shape[2] // 256
    qs = [q_ref[0, :, g * 256:(g + 1) * 256] for g in range(heads)]
    res = _flash_cols(qs, k_ref, vt_ref, s_ref, acc_ref, 256, V_DIM, pl.program_id(2), tq, tk,
                      lambda r: r)
    for g, (acc, l) in enumerate(res):
        o_ref[0, :, g * V_DIM:(g + 1) * V_DIM] = (acc / l).T.astype(o_ref.dtype)


def _flash_diff(qa, k16, v16t, lam_refs, gsub_col, lam_init, b, s, tk):
    tq = 2 * tk
    heads = FLASH_DIFF_HEADS
    gw = heads * DV_A
    vec = pl.BlockSpec((1, DQK), lambda bi, hi, qi: (0, 0))
    return pl.pallas_call(
        functools.partial(_flash_diff_kernel, tq, tk, lam_init),
        grid=(b, H_A // heads, s // tq),
        in_specs=[
            pl.BlockSpec((1, tq, gw), lambda bi, hi, qi: (bi, qi, hi)),
            pl.BlockSpec((1, s, gw), lambda bi, hi, qi: (bi, 0, hi)),
            pl.BlockSpec((1, s // tk, heads * (DV_A + ONES_ROWS), tk), lambda bi, hi, qi: (bi, 0, hi, 0)),
            vec, vec, vec, vec,
            pl.BlockSpec((DV_A, 1), lambda bi, hi, qi: (0, 0)),
        ],
        out_specs=pl.BlockSpec((1, tq, gw), lambda bi, hi, qi: (bi, qi, hi)),
        out_shape=jax.ShapeDtypeStruct((b, s, W_A), BF16),
        scratch_shapes=[pltpu.VMEM((2, heads, tk, 2 * tq), F32), pltpu.VMEM((heads, DV_A + ONES_ROWS, 2 * tq), F32)],
        compiler_params=_params("arbitrary", "arbitrary", "arbitrary"),
        name="flash_diff",
    )(qa, k16, v16t, *lam_refs, gsub_col)


def _flash_mla(qcat, kcat, vmlat, b, s, tk):
    tq = 2 * tk
    g = FLASH_MLA_HEADS
    return pl.pallas_call(
        functools.partial(_flash_mla_kernel, tq, tk),
        grid=(b, H_B // g, s // tq),
        in_specs=[
            pl.BlockSpec((1, tq, g * 256), lambda bi, hi, qi: (bi, qi, hi)),
            pl.BlockSpec((1, s, g * 256), lambda bi, hi, qi: (bi, 0, hi)),
            pl.BlockSpec((1, s // tk, g * (V_DIM + ONES_ROWS), tk), lambda bi, hi, qi: (bi, 0, hi, 0)),
        ],
        out_specs=pl.BlockSpec((1, tq, g * V_DIM), lambda bi, hi, qi: (bi, qi, hi)),
        out_shape=jax.ShapeDtypeStruct((b, s, W_B), BF16),
        scratch_shapes=[pltpu.VMEM((2, g, tk, tq), F32), pltpu.VMEM((g, V_DIM + ONES_ROWS, tq), F32)],
        compiler_params=_params("arbitrary", "arbitrary", "arbitrary"),
        name="flash_mla",
    )(qcat, kcat, vmlat)


def _dec_update(s, v, m_ref, l_ref, acc_ref, g):
    m_new, l, acc = _online_softmax_step(s, v, m_ref[g], l_ref[g], acc_ref[g])
    m_ref[g] = m_new
    l_ref[g] = l
    acc_ref[g] = acc


def _dec_diff_kernel(n, lam_init, q_ref, k_ref, v_ref, kn_ref, vn_ref, lq1_ref, lk1_ref, lq2_ref,
                     lk2_ref, gsub_ref, o_ref, qbd_ref, m_ref, l_ref, acc_ref):
    j = pl.program_id(1)
    gw = 4 * DV_A
    chains = [(sq, g) for sq in range(DEC_DIFF_SEQS) for g in range(2)]

    @pl.when(j == 0)
    def _():
        for c, (sq, g) in enumerate(chains):
            qg = q_ref[sq * n:(sq + 1) * n, g * gw:(g + 1) * gw]
            lane = lax.broadcasted_iota(jnp.int32, qg.shape, 1)
            blocks = [jnp.where((lane >> DQK_SHIFT) == blk, qg, jnp.zeros_like(qg))
                      for blk in range(8)]
            qbd_ref[c] = jnp.concatenate(blocks, axis=0)
        m_ref[...] = jnp.full(m_ref.shape, NEG_INF, F32)
        l_ref[...] = jnp.zeros(l_ref.shape, F32)
        acc_ref[...] = jnp.zeros(acc_ref.shape, F32)

    def group_lanes(ref, g):
        keys = ref.shape[0] // H_A
        return jnp.concatenate(
            [ref[pl.ds(g * 4 + hl, keys, stride=H_A), :] for hl in range(4)], axis=1).astype(BF16)

    def attend(kview, vview):
        scores = [_dot_nt(qbd_ref[c], group_lanes(kview(sq), g)) for c, (sq, g) in enumerate(chains)]
        for c, (sq, g) in enumerate(chains):
            _dec_update(scores[c], group_lanes(vview(sq), g), m_ref, l_ref, acc_ref, c)

    attend(lambda sq: k_ref.at[sq], lambda sq: v_ref.at[sq])

    @pl.when(j == pl.num_programs(1) - 1)
    def _():
        lam = _lambda_full(lq1_ref, lk1_ref, lq2_ref, lk2_ref, lam_init)
        new = n * H_A
        attend(lambda sq: kn_ref.at[sq * new:(sq + 1) * new], lambda sq: vn_ref.at[sq * new:(sq + 1) * new])
        for c, (sq, g) in enumerate(chains):
            o = acc_ref[c] / l_ref[c]
            for hl in range(4):
                cols = slice(hl * DV_A, (hl + 1) * DV_A)
                o0 = o[(2 * hl) * n:(2 * hl + 1) * n, cols]
                o1 = o[(2 * hl + 1) * n:(2 * hl + 2) * n, cols]
                od = _rms(o0 - lam * o1, SUBLN_EPS) * gsub_ref[...] * (1.0 - lam_init)
                hd = g * 4 + hl
                o_ref[sq * n:(sq + 1) * n, hd * DV_A:(hd + 1) * DV_A] = od.astype(o_ref.dtype)


def _dec_diff(qa, cache_k, cache_v, k_new, v_new, lam_refs, gsub, lam_init, b, n):
    past = cache_k.shape[1] // H_A
    tk = DEC_DIFF_KEYS
    sq = DEC_DIFF_SEQS
    vec = pl.BlockSpec((1, DQK), lambda bi, j: (0, 0))
    rows = pl.BlockSpec((sq * n, W_A), lambda bi, j: (bi, 0))
    new_rows = pl.BlockSpec((sq * n * H_A, DV_A), lambda bi, j: (bi, 0))
    cache_tile = pl.BlockSpec((sq, tk * H_A, DV_A), lambda bi, j: (bi, j, 0))
    return pl.pallas_call(
        functools.partial(_dec_diff_kernel, n, lam_init),
        grid=(b // sq, past // tk),
        in_specs=[
            rows, cache_tile, cache_tile, new_rows, new_rows, vec, vec, vec, vec,
            pl.BlockSpec((1, DV_A), lambda bi, j: (0, 0)),
        ],
        out_specs=rows,
        out_shape=jax.ShapeDtypeStruct((b * n, W_A), BF16),
        scratch_shapes=[
            pltpu.VMEM((2 * sq, 8 * n, 4 * DV_A), BF16),
            pltpu.VMEM((2 * sq, 8 * n, 1), F32),
            pltpu.VMEM((2 * sq, 8 * n, 1), F32),
            pltpu.VMEM((2 * sq, 8 * n, 4 * DV_A), F32),
        ],
        compiler_params=_params("arbitrary", "arbitrary"),
        name="dec_diff",
    )(qa, cache_k, cache_v, k_new, v_new, *lam_refs, gsub)


def _dec_mla_kernel(n, qlat_ref, qpe_ref, ckv_ref, kpet_ref, ckvn_ref, kpen_ref, wuv_ref, o_ref,
                    ql_ref, qp_ref, m_ref, l_ref, acc_ref):
    j = pl.program_id(1)
    seqs = range(DEC_SEQS)

    @pl.when(j == 0)
    def _():
        for sq in seqs:
            for hd in range(H_B):
                rows = slice(sq * n, (sq + 1) * n)
                ql_ref[sq, hd * n:(hd + 1) * n, :] = qlat_ref[rows, hd * KV_LORA:(hd + 1) * KV_LORA]
                qp_ref[sq, hd * n:(hd + 1) * n, :] = qpe_ref[rows, hd * 128:(hd + 1) * 128]
        m_ref[...] = jnp.full(m_ref.shape, NEG_INF, F32)
        l_ref[...] = jnp.zeros(l_ref.shape, F32)
        acc_ref[...] = jnp.zeros(acc_ref.shape, F32)

    def attend(ckv_of, pe_scores_of):
        ckvs = [ckv_of(sq).astype(BF16) for sq in seqs]
        scores = [_dot_nt(ql_ref[sq], ckvs[sq]) + pe_scores_of(sq, qp_ref[sq, :, :ROPE_DIM]) for sq in seqs]
        for sq in seqs:
            _dec_update(scores[sq], ckvs[sq], m_ref, l_ref, acc_ref, sq)

    attend(lambda sq: ckv_ref[sq], lambda sq, qp: _dot(qp, kpet_ref[sq].astype(BF16)))

    @pl.when(j == pl.num_programs(1) - 1)
    def _():
        attend(lambda sq: ckvn_ref[sq * n:(sq + 1) * n, :],
               lambda sq, qp: _dot_nt(qp, kpen_ref[sq * n:(sq + 1) * n, :].astype(BF16)))
        for sq in seqs:
            o = (acc_ref[sq] / l_ref[sq]).astype(BF16)
            for hd in range(H_B):
                o_ref[sq * n:(sq + 1) * n, hd * V_DIM:(hd + 1) * V_DIM] = _dot(
                    o[hd * n:(hd + 1) * n], wuv_ref[hd]).astype(o_ref.dtype)


def _dec_mla(qlat, qpe, cache_ckv, cache_kpe_t, ckv_new, kpe_new, wuv_h, b, n):
    past = cache_ckv.shape[1]
    tk = DEC_MLA_KEYS
    sq = DEC_SEQS
    return pl.pallas_call(
        functools.partial(_dec_mla_kernel, n),
        grid=(b // sq, past // tk),
        in_specs=[
            pl.BlockSpec((sq * n, H_B * KV_LORA), lambda bi, j: (bi, 0)),
            pl.BlockSpec((sq * n, H_B * 128), lambda bi, j: (bi, 0)),
            pl.BlockSpec((sq, tk, KV_LORA), lambda bi, j: (bi, j, 0)),
            pl.BlockSpec((sq, ROPE_DIM, tk), lambda bi, j: (bi, 0, j)),
            pl.BlockSpec((sq * n, KV_LORA), lambda bi, j: (bi, 0)),
            pl.BlockSpec((sq * n, ROPE_DIM), lambda bi, j: (bi, 0)),
            pl.BlockSpec((H_B, KV_LORA, V_DIM), lambda bi, j: (0, 0, 0)),
        ],
        out_specs=pl.BlockSpec((sq * n, W_B), lambda bi, j: (bi, 0)),
        out_shape=jax.ShapeDtypeStruct((b * n, W_B), BF16),
        scratch_shapes=[
            pltpu.VMEM((sq, H_B * n, KV_LORA), BF16),
            pltpu.VMEM((sq, H_B * n, 128), BF16),
            pltpu.VMEM((sq, H_B * n, 1), F32),
            pltpu.VMEM((sq, H_B * n, 1), F32),
            pltpu.VMEM((sq, H_B * n, KV_LORA), F32),
        ],
        compiler_params=_params("arbitrary", "arbitrary"),
        name="dec_mla",
    )(qlat, qpe, cache_ckv, cache_kpe_t, ckv_new, kpe_new, wuv_h)


def _out_proj_kernel(oa_ref, ob_ref, x_ref, shift_ref, scale_ref, gate_ref, gn_ref, wgate_ref, w_ref,
                     o_ref):
    nb = x_ref.shape[0]
    for _, rs, ts in _token_chains(x_ref):
        sg = _silu(_dot_nt(_prenorm_tokens(x_ref, rs, gn_ref, scale_ref, shift_ref), wgate_ref[...]))
        oa = (oa_ref[ts, :].astype(F32) * sg[:, :W_A]).astype(BF16)
        ob = (ob_ref[ts, :].astype(F32) * sg[:, W_A:]).astype(BF16)
        y = _dot(oa, w_ref[:W_A, :]) + _dot(ob, w_ref[W_A:, :])
        o_ref[:, rs, :] = x_ref[:, rs, :] + gate_ref[...] * y.reshape(nb, -1, D_MODEL)


def _pool_kernel(tiles_per_seq, pos0, carry, x_ref, shift_ref, scale_ref, gate_ref, gn_ref, pre_ref,
                 win_ref, wgrp_ref, sp_ref, wout_ref, gf_ref, y_ref, st_ref, uext_ref):
    nb, rows, _ = x_ref.shape
    t = pl.program_id(0) % tiles_per_seq
    if carry:
        @pl.when(t == 0)
        def _():
            uext_ref[:, :POOL_PREFIX, :] = pre_ref[...]
    else:
        uext_ref[:, :POOL_PREFIX, :] = pre_ref[...]

    for r0, r1 in _row_chains(rows):
        n = r1 - r0
        x = x_ref[:, r0:r1, :]
        h = _prenorm(x, gn_ref[...], scale_ref[...], shift_ref[...])
        z = _dot(h.reshape(nb * n, D_MODEL).astype(BF16), win_ref[...])
        u = z[:, :W_C].reshape(nb, n, W_C)
        sgate = _silu(z[:, W_C:])
        uext_ref[:, POOL_PREFIX + r0:POOL_PREFIX + r1, :] = u
        pos = pos0 + t * rows + r0 + lax.broadcasted_iota(jnp.int32, (1, n, 1), 1)
        parts = []
        for g, w in enumerate(POOL_WINDOWS):
            cols = slice(g * POOL_GW, (g + 1) * POOL_GW)
            ws = u[:, :, cols]
            for k in range(1, w):
                ws = ws + uext_ref[:, POOL_PREFIX + r0 - k:POOL_PREFIX + r1 - k, cols]
            cnt = jnp.minimum(pos + 1, w).astype(F32)
            dgrp = (ws / cnt - u[:, :, cols]).reshape(nb * n, POOL_GW).astype(BF16)
            parts.append(_dot(dgrp, wgrp_ref[g]))
        m = jnp.concatenate(parts, axis=1) * sp_ref[...]
        y = _dot((m * sgate).astype(BF16), wout_ref[...])
        x2 = x + gate_ref[...] * y.reshape(nb, n, D_MODEL)
        y_ref[:, r0:r1, :] = _rms(x2, EPS) * gf_ref[...]
    tail = uext_ref[:, rows:rows + POOL_PREFIX, :]
    st_ref[...] = tail
    if carry:
        uext_ref[:, :POOL_PREFIX, :] = tail


def _rot_half_cols(w):
    half = ROPE_DIM // 2
    return jnp.concatenate([-w[..., half:], w[..., :half]], axis=-1)


def _rope_tables(start, count, repeat=1):
    half = ROPE_DIM // 2
    inv = ROPE_THETA ** (-np.arange(half, dtype=np.float64) / half)
    ang = np.arange(start, start + count, dtype=np.float64)[:, None] * inv[None, :]
    zero = np.zeros((count, 128 - ROPE_DIM))
    tabs = [np.tile(np.concatenate([t, t, zero], axis=1), (repeat, 1)) for t in (np.cos(ang), np.sin(ang))]
    return tuple(jnp.asarray(t, F32) for t in tabs)


def _group(x3, mod, cos, sin, sample, cache, w, tm_att, tm_pool):
    bt, st, _ = x3.shape
    tokens = bt * st
    lam_init = _lambda_init(0)

    if sample:
        nb, rows, tps = SAMPLE_TILE_SEQS, st, 1
        n_tiles = bt // nb
        x_spec = pl.BlockSpec((nb, rows, D_MODEL), lambda i: (i, 0, 0))

        def mod_spec(layer, comp):
            blk = (layer * 3 + comp) * MOD_ROWS // nb
            return pl.BlockSpec((nb, 1, D_MODEL), lambda i: (blk + i, 0, 0))
    else:
        nb, rows = 1, tm_att
        tps = st // rows
        n_tiles = bt * tps
        x_spec = pl.BlockSpec((1, rows, D_MODEL), lambda i: (i // tps, i % tps, 0))

        def mod_spec(layer, comp):
            base = (layer * 3 + comp) * MOD_ROWS + 16
            return pl.BlockSpec((1, 1, D_MODEL), lambda i: (base + i // tps, 0, 0))

    tm = nb * rows

    def tok(width):
        return pl.BlockSpec((tm, width), lambda i: (i, 0))

    def tok_shape(width, dt):
        return jax.ShapeDtypeStruct((tokens, width), dt)

    vec = lambda width: pl.BlockSpec((1, width), lambda i: (0, 0))
    if sample:
        tab = pl.BlockSpec((tm, 128), lambda i: (0, 0))
    else:
        tab = pl.BlockSpec((tm, 128), lambda i: (i % tps, 0))

    heads_spec = pl.BlockSpec((tm * H_A, DV_A), lambda i: (i, 0))
    heads_shape = jax.ShapeDtypeStruct((tokens * H_A, DV_A), F32)
    vt_rows = H_A * (DV_A + ONES_ROWS)
    key_tile = min(tm, CHAIN_ROWS)
    vt_spec = pl.BlockSpec((1, tm // key_tile, vt_rows, key_tile), lambda i: (i // tps, i % tps, 0, 0))
    vt_shape = jax.ShapeDtypeStruct((bt, st // key_tile, vt_rows, key_tile), BF16)
    qkv_out = pl.pallas_call(
        _in_qkv_kernel,
        grid=(n_tiles,),
        in_specs=[x_spec, mod_spec(0, 0), mod_spec(0, 1), vec(D_MODEL), _resident((3 * W_A, D_MODEL))],
        out_specs=[tok(W_A), heads_spec, heads_spec] + ([] if sample else [tok(W_A), vt_spec]),
        out_shape=[tok_shape(W_A, BF16), heads_shape, heads_shape]
        + ([] if sample else [tok_shape(W_A, BF16), vt_shape]),
        compiler_params=_params("arbitrary"),
        name="in_qkv",
    )(x3, mod, mod, w["g_norm0"], w["wqkv"])
    qa, k32, v32 = qkv_out[:3]

    lat_in = [x_spec, mod_spec(0, 0), mod_spec(0, 1), vec(D_MODEL), tab, tab,
              _resident((Q_LORA + KV_LORA, D_MODEL)), _resident((256, D_MODEL)),
              vec(Q_LORA), vec(KV_LORA),
              _resident((Q_LORA, H_B * 128)), _resident((Q_LORA, H_B * 128)), _resident((Q_LORA, H_B * 128))]
    lat_args = [x3, mod, mod, w["g_norm0"], cos, sin, w["wc"], w["wkpe"], w["g_qa"], w["g_kva"],
                w["wuq"], w["wuqr"], w["wuqrr"]]
    lam_refs = (w["lam_q1"], w["lam_k1"], w["lam_q2"], w["lam_k2"])

    if sample:
        ckv32, kpe32, qlat, qpe = pl.pallas_call(
            _in_lat_sample_kernel,
            grid=(n_tiles,),
            in_specs=lat_in + [_resident((H_B, 128, KV_LORA))],
            out_specs=[tok(KV_LORA), tok(ROPE_DIM), tok(H_B * KV_LORA), tok(H_B * 128)],
            out_shape=[tok_shape(KV_LORA, F32), tok_shape(ROPE_DIM, F32),
                       tok_shape(H_B * KV_LORA, BF16), tok_shape(H_B * 128, BF16)],
            compiler_params=_params("arbitrary"),
            name="in_lat_sample",
        )(*lat_args, w["wukt"])
        cache_k, cache_v, cache_ckv, cache_kpe = cache
        o_a = _dec_diff(qa, cache_k, cache_v, k32, v32, lam_refs, w["g_subln"], lam_init, bt, st)
        o_b = _dec_mla(qlat, qpe, cache_ckv, cache_kpe, ckv32, kpe32, w["wuv_h"], bt, st)
    else:
        ckv32, kpe32, qcat, kcat, vmla = pl.pallas_call(
            _in_lat_prompt_kernel,
            grid=(n_tiles,),
            in_specs=lat_in + [_resident((KV_LORA, H_B * 128)), _resident((KV_LORA, H_B * 128))],
            out_specs=[tok(KV_LORA), tok(ROPE_DIM), tok(H_B * 256), tok(H_B * 256), vt_spec],
            out_shape=[tok_shape(KV_LORA, F32), tok_shape(ROPE_DIM, F32),
                       tok_shape(H_B * 256, BF16), tok_shape(H_B * 256, BF16), vt_shape],
            compiler_params=_params("arbitrary"),
            name="in_lat_prompt",
        )(*lat_args, w["wuk"], w["wuv"])
        k16, v16t = qkv_out[3:]
        o_a = _flash_diff(qa.reshape(bt, st, W_A), k16.reshape(bt, st, W_A), v16t, lam_refs,
                          w["g_subln"].reshape(DV_A, 1), lam_init, bt, st, key_tile).reshape(tokens, W_A)
        o_b = _flash_mla(qcat.reshape(bt, st, H_B * 256), kcat.reshape(bt, st, H_B * 256),
                         vmla, bt, st, key_tile).reshape(tokens, W_B)

    x1 = pl.pallas_call(
        _out_proj_kernel,
        grid=(n_tiles,),
        in_specs=[tok(W_A), tok(W_B), x_spec, mod_spec(0, 0), mod_spec(0, 1), mod_spec(0, 2), vec(D_MODEL),
                  _resident((W_ATT, D_MODEL)), _resident((W_ATT, D_MODEL))],
        out_specs=x_spec,
        out_shape=jax.ShapeDtypeStruct(x3.shape, F32),
        compiler_params=_params("arbitrary"),
        name="out_proj",
    )(o_a, o_b, x3, mod, mod, mod, w["g_norm0"], w["wgate"], w["wout_att"])

    if sample:
        p_nb, p_rows, p_tiles, p_tps = nb, st, n_tiles, 1
        px_spec = x_spec
        pmod = mod_spec
        pre_spec = pl.BlockSpec((nb, POOL_PREFIX, W_C), lambda i: (i, 0, 0))
        st_spec = pl.BlockSpec((nb, POOL_PREFIX, W_C), lambda i: (i, 0, 0))
        pos0 = cache[2].shape[1]
        prefix = w["pool_prefix"]
    else:
        p_nb, p_rows = 1, tm_pool
        p_tps = st // p_rows
        p_tiles = bt * p_tps
        px_spec = pl.BlockSpec((1, p_rows, D_MODEL), lambda i: (i // p_tps, i % p_tps, 0))

        def pmod(layer, comp):
            base = (layer * 3 + comp) * MOD_ROWS + 16
            return pl.BlockSpec((1, 1, D_MODEL), lambda i: (base + i // p_tps, 0, 0))

        pre_spec = pl.BlockSpec((1, POOL_PREFIX, W_C), lambda i: (0, 0, 0))
        st_spec = pl.BlockSpec((1, POOL_PREFIX, W_C), lambda i: (i // p_tps, 0, 0))
        pos0 = 0
        prefix = jnp.zeros((1, POOL_PREFIX, W_C), F32)

    y, state = pl.pallas_call(
        functools.partial(_pool_kernel, p_tps, pos0, not sample),
        grid=(p_tiles,),
        in_specs=[px_spec, pmod(1, 0), pmod(1, 1), pmod(1, 2), vec(D_MODEL), pre_spec,
                  _resident((D_MODEL, 2 * W_C)), _resident((N_POOL_GROUPS, POOL_GW, POOL_GW)),
                  vec(W_C), _resident((W_C, D_MODEL)), vec(D_MODEL)],
        out_specs=[px_spec, st_spec],
        out_shape=[jax.ShapeDtypeStruct(x3.shape, F32),
                   jax.ShapeDtypeStruct((bt, POOL_PREFIX, W_C), F32)],
        scratch_shapes=[pltpu.VMEM((p_nb, POOL_PREFIX + p_rows, W_C), F32)],
        compiler_params=_params("arbitrary"),
        name="pool",
    )(x1, mod, mod, mod, w["g_norm1"], prefix, w["win_pool"], w["wgrp"], w["scale_pool"],
      w["wout_pool"], w["g_final"])

    return (y,
            k32.reshape(1, bt, st, H_A, DV_A), v32.reshape(1, bt, st, H_A, DV_A),
            ckv32.reshape(1, bt, st, KV_LORA), kpe32.reshape(1, bt, st, ROPE_DIM),
            state[None, :, 1:, :])


def kernel(x_prompt, x_sample, c_prompt, c_sample, cache_diff_k, cache_diff_v, cache_mla_ckv, cache_mla_kpe, state_pool, g_norm, w_ada, b_ada, w_in_att, w_out_att, lam_q1, lam_k1, lam_q2, lam_k2, g_subln, g_qa, w_uq, w_uqr, w_uk, w_uv, g_kva, w_in_pool, w_grp_pool, scale_pool, w_out_pool, g_final):
    b_p, s_p, _ = x_prompt.shape
    b_s, n_s, _ = x_sample.shape
    past = cache_diff_k.shape[2]

    c_all = jnp.concatenate(
        [c_sample, c_prompt, jnp.zeros((MOD_ROWS - b_s - b_p, D_MODEL), F32)], axis=0)
    mod = _ada(c_all, w_ada, b_ada).reshape(-1, 1, D_MODEL)

    w_in_t = w_in_att[0].T
    kpe_w = w_in_t[4 * W_A:4 * W_A + ROPE_DIM].T
    zpad = jnp.zeros((D_MODEL, 128 - ROPE_DIM), F32)
    uqr = w_uqr[0]
    pad_heads = lambda a: jnp.pad(a, ((0, 0), (0, 0), (0, 128 - ROPE_DIM))).reshape(Q_LORA, H_B * 128)
    w = {
        "g_norm0": g_norm[0:1], "g_norm1": g_norm[1:2],
        "wqkv": w_in_t[:3 * W_A].astype(BF16),
        "wc": w_in_t[3 * W_A:4 * W_A].astype(BF16),
        "wkpe": jnp.concatenate([kpe_w, zpad, _rot_half_cols(kpe_w), zpad], axis=1).T.astype(BF16),
        "wgate": w_in_t[4 * W_A + ROPE_DIM:].astype(BF16),
        "g_qa": g_qa, "g_kva": g_kva,
        "wuq": w_uq[0].reshape(Q_LORA, H_B * NOPE_DIM).astype(BF16),
        "wuqr": pad_heads(uqr).astype(BF16),
        "wuqrr": pad_heads(_rot_half_cols(uqr)).astype(BF16),
        "wuk": w_uk[0].reshape(KV_LORA, H_B * NOPE_DIM).astype(BF16),
        "wuv": w_uv[0].reshape(KV_LORA, H_B * V_DIM).astype(BF16),
        "wukt": jnp.transpose(w_uk[0], (1, 2, 0)).astype(BF16),
        "wuv_h": jnp.transpose(w_uv[0], (1, 0, 2)).astype(BF16),
        "wout_att": w_out_att[0].astype(BF16),
        "lam_q1": lam_q1, "lam_k1": lam_k1, "lam_q2": lam_q2, "lam_k2": lam_k2,
        "g_subln": g_subln,
        "win_pool": w_in_pool[0].astype(BF16),
        "wgrp": w_grp_pool[0].astype(BF16),
        "scale_pool": scale_pool,
        "wout_pool": w_out_pool[0].astype(BF16),
        "g_final": g_final.reshape(1, D_MODEL),
        "pool_prefix": jnp.pad(state_pool[0], ((0, 0), (1, 0), (0, 0))),
    }

    cos_p, sin_p = _rope_tables(0, s_p)
    cos_s, sin_s = _rope_tables(past, n_s, SAMPLE_TILE_SEQS)

    out_p = _group(x_prompt, mod, cos_p, sin_p, False, None, w, 2 * CHAIN_ROWS, 2 * CHAIN_ROWS)
    cache = (cache_diff_k[0].reshape(b_s, past * H_A, DV_A), cache_diff_v[0].reshape(b_s, past * H_A, DV_A),
             cache_mla_ckv[0], jnp.swapaxes(cache_mla_kpe[0], 1, 2))
    out_s = _group(x_sample, mod, cos_s, sin_s, True, cache, w, 0, 0)
    return (out_p[0], out_s[0]) + out_p[1:] + out_s[1:]
```

```python
import functools
import math

import jax
import jax.numpy as jnp
import numpy as np
from jax import lax
from jax.experimental import pallas as pl
from jax.experimental.pallas import tpu as pltpu

F32 = jnp.float32
BF16 = jnp.bfloat16

D_MODEL = 2048
CHUNK = 64
CHUNK_SHIFT = 6
H_A = 8
DQK = 64
DQK_SHIFT = 6
DV_A = 2 * DQK
W_A = H_A * DV_A
H_B = 8
Q_LORA = 512
KV_LORA = 512
NOPE_DIM = 128
ROPE_DIM = 64
V_DIM = 128
W_B = H_B * V_DIM
ROPE_THETA = 10000.0
W_ATT = W_A + W_B
POOL_WINDOWS = (2, 4, 8, 16)
N_POOL_GROUPS = 4
W_C = D_MODEL
POOL_GW = W_C // N_POOL_GROUPS
POOL_STATE = max(POOL_WINDOWS) - 1
POOL_PREFIX = POOL_STATE + 1
EPS = 1e-6
SUBLN_EPS = 1e-5
NEG_INF = -1e30
LOG2E = math.log2(math.e)
DIFF_QSCALE = DQK ** -0.5 * LOG2E
MLA_QSCALE = (NOPE_DIM + ROPE_DIM) ** -0.5 * LOG2E
MOD_ROWS = 32
FLASH_DIFF_HEADS = 4
FLASH_MLA_HEADS = 4
DEC_SEQS = 2
DEC_MLA_KEYS = 2048
DEC_DIFF_SEQS = 1
DEC_DIFF_KEYS = 2048
CHAIN_ROWS = 256
ONES_ROWS = 16
SAMPLE_TILE_SEQS = 8
VMEM_LIMIT = 56 * 1024 * 1024


def _lambda_init(layer_idx):
    return 0.8 - 0.6 * math.exp(-0.3 * layer_idx)


def _rms(x, eps):
    return x * lax.rsqrt(jnp.mean(x * x, axis=-1, keepdims=True) + eps)


def _silu(x):
    return x * (1.0 / (1.0 + jnp.exp(-x)))


def _dot(a, b):
    return jnp.dot(a, b, preferred_element_type=F32)


def _dot_nt(a, b):
    return lax.dot_general(a, b, (((1,), (1,)), ((), ())), preferred_element_type=F32)


def _resident(shape):
    nd = len(shape)
    return pl.BlockSpec(shape, lambda *_: (0,) * nd, pipeline_mode=pl.Buffered(1))


def _resident_rows(row0, rows, cols):
    return pl.BlockSpec((pl.Element(rows), pl.Element(cols)), lambda *_: (row0, 0),
                        pipeline_mode=pl.Buffered(1))


def _params(*sem):
    return pltpu.CompilerParams(dimension_semantics=sem, vmem_limit_bytes=VMEM_LIMIT)


def _ada_kernel(c_ref, w_ref, b_ref, o_ref):
    cond = _silu(c_ref[...]).astype(BF16)
    w = w_ref[0].astype(BF16)
    o_ref[:, 0, :] = _dot(cond, w) + b_ref[0, 0]


def _ada(c_all, w_ada, b_ada):
    depth = w_ada.shape[0]
    tn = 1024
    per = D_MODEL // tn
    return pl.pallas_call(
        _ada_kernel,
        grid=(depth, 3 * per),
        in_specs=[
            pl.BlockSpec((MOD_ROWS, D_MODEL), lambda l, j: (0, 0)),
            pl.BlockSpec((1, D_MODEL, tn), lambda l, j: (l, 0, j)),
            pl.BlockSpec((1, 1, 1, tn), lambda l, j: (l, j // per, 0, j % per)),
        ],
        out_specs=pl.BlockSpec((MOD_ROWS, 1, tn), lambda l, j: (l * 3 + j // per, 0, j % per)),
        out_shape=jax.ShapeDtypeStruct((depth * 3 * MOD_ROWS, 1, D_MODEL), F32),
        compiler_params=_params("arbitrary", "arbitrary"),
        name="ada",
    )(c_all, w_ada, b_ada.reshape(depth, 3, 1, D_MODEL))


def _row_chains(rows):
    step = min(rows, CHAIN_ROWS)
    return [(r, r + step) for r in range(0, rows, step)]


def _prenorm(x, gn, scale, shift):
    return _rms(x, EPS) * gn * (1.0 + scale) + shift


def _store_vt(vt_ref, tile, v, heads, dv):
    vt = v.T.astype(BF16)
    ones = jnp.ones((ONES_ROWS, v.shape[0]), BF16)
    for hd in range(heads):
        base = hd * (dv + ONES_ROWS)
        vt_ref[0, tile, base:base + dv, :] = vt[hd * dv:(hd + 1) * dv]
        vt_ref[0, tile, base + dv:base + dv + ONES_ROWS, :] = ones


def _token_chains(x_ref):
    nb, rows, _ = x_ref.shape
    chains = _row_chains(rows)
    assert nb == 1 or len(chains) == 1
    return [(c, slice(r0, r1), slice(nb * r0, nb * r1)) for c, (r0, r1) in enumerate(chains)]


def _prenorm_tokens(x_ref, rs, gn_ref, scale_ref, shift_ref):
    x = x_ref[:, rs, :]
    h = _prenorm(x, gn_ref[...], scale_ref[...], shift_ref[...])
    return h.reshape(x.shape[0] * x.shape[1], D_MODEL).astype(BF16)


def _in_qkv_kernel(x_ref, shift_ref, scale_ref, gn_ref, wqkv_ref, qa_ref, k32_ref, v32_ref, *flash_refs):
    for c, rs, ts in _token_chains(x_ref):
        z = _dot_nt(_prenorm_tokens(x_ref, rs, gn_ref, scale_ref, shift_ref), wqkv_ref[...])
        qa_ref[ts, :] = (z[:, :W_A] * DIFF_QSCALE).astype(BF16)
        ka = z[:, W_A:2 * W_A]
        va = z[:, 2 * W_A:]
        for hd in range(H_A):
            head_rows = pl.ds(ts.start * H_A + hd, ts.stop - ts.start, stride=H_A)
            k32_ref[head_rows, :] = ka[:, hd * DV_A:(hd + 1) * DV_A]
            v32_ref[head_rows, :] = va[:, hd * DV_A:(hd + 1) * DV_A]
        if flash_refs:
            k16_ref, vt_ref = flash_refs
            k16_ref[ts, :] = ka.astype(BF16)
            _store_vt(vt_ref, c, va, H_A, DV_A)


def _latent_chain(x_ref, rs, ts, shift_ref, scale_ref, gn_ref, cos_ref, sin_ref, wc_ref, wkpe_ref,
                  gqa_ref, gkva_ref, wuq_ref, wuqr_ref, wuqrr_ref, ckv32_ref, kpe32_ref):
    hb = _prenorm_tokens(x_ref, rs, gn_ref, scale_ref, shift_ref)
    zc = _dot_nt(hb, wc_ref[...])
    cq = (_rms(zc[:, :Q_LORA], EPS) * gqa_ref[...]).astype(BF16)
    ckv = _rms(zc[:, Q_LORA:], EPS) * gkva_ref[...]
    ckv32_ref[ts, :] = ckv
    cos = cos_ref[ts, :]
    sin = sin_ref[ts, :]
    zk = _dot_nt(hb, wkpe_ref[...])
    kpe_pad = zk[:, :128] * cos + zk[:, 128:] * sin
    kpe32_ref[ts, :] = kpe_pad[:, :ROPE_DIM]
    q_nope = _dot(cq, wuq_ref[...]) * MLA_QSCALE
    cos8 = jnp.tile(cos, (1, H_B))
    sin8 = jnp.tile(sin, (1, H_B))
    q_pe = (_dot(cq, wuqr_ref[...]) * cos8 + _dot(cq, wuqrr_ref[...]) * sin8) * MLA_QSCALE
    return ckv, kpe_pad, q_nope, q_pe


def _in_lat_prompt_kernel(x_ref, shift_ref, scale_ref, gn_ref, cos_ref, sin_ref, wc_ref, wkpe_ref,
                          gqa_ref, gkva_ref, wuq_ref, wuqr_ref, wuqrr_ref, wuk_ref, wuv_ref,
                          ckv32_ref, kpe32_ref, qcat_ref, kcat_ref, vmla_ref):
    for c, rs, ts in _token_chains(x_ref):
        ckv, kpe_pad, q_nope, q_pe = _latent_chain(
            x_ref, rs, ts, shift_ref, scale_ref, gn_ref, cos_ref, sin_ref, wc_ref, wkpe_ref,
            gqa_ref, gkva_ref, wuq_ref, wuqr_ref, wuqrr_ref, ckv32_ref, kpe32_ref)
        ckvb = ckv.astype(BF16)
        k_nope = _dot(ckvb, wuk_ref[...])
        _store_vt(vmla_ref, c, _dot(ckvb, wuv_ref[...]), H_B, V_DIM)
        kpe_b = kpe_pad.astype(BF16)
        for hd in range(H_B):
            sl = slice(hd * 128, (hd + 1) * 128)
            qcat_ref[ts, hd * 256:hd * 256 + 128] = q_nope[:, sl].astype(BF16)
            qcat_ref[ts, hd * 256 + 128:(hd + 1) * 256] = q_pe[:, sl].astype(BF16)
            kcat_ref[ts, hd * 256:hd * 256 + 128] = k_nope[:, sl].astype(BF16)
            kcat_ref[ts, hd * 256 + 128:(hd + 1) * 256] = kpe_b


def _in_lat_sample_kernel(x_ref, shift_ref, scale_ref, gn_ref, cos_ref, sin_ref, wc_ref, wkpe_ref,
                          gqa_ref, gkva_ref, wuq_ref, wuqr_ref, wuqrr_ref, wukt_ref,
                          ckv32_ref, kpe32_ref, qlat_ref, qpe_ref):
    for _, rs, ts in _token_chains(x_ref):
        _, _, q_nope, q_pe = _latent_chain(
            x_ref, rs, ts, shift_ref, scale_ref, gn_ref, cos_ref, sin_ref, wc_ref, wkpe_ref,
            gqa_ref, gkva_ref, wuq_ref, wuqr_ref, wuqrr_ref, ckv32_ref, kpe32_ref)
        qn = q_nope.astype(BF16)
        for hd in range(H_B):
            qlat_ref[ts, hd * KV_LORA:(hd + 1) * KV_LORA] = _dot(
                qn[:, hd * 128:(hd + 1) * 128], wukt_ref[hd]).astype(BF16)
        qpe_ref[ts, :] = q_pe.astype(BF16)


def _online_softmax_step(s, v, m, l, acc):
    m_new = jnp.maximum(m, jnp.max(s, axis=1, keepdims=True))
    alpha = jnp.exp2(m - m_new)
    p = jnp.exp2(s - m_new)
    l = alpha * l + jnp.sum(p, axis=1, keepdims=True)
    acc = alpha * acc + _dot(p.astype(BF16), v)
    return m_new, l, acc


def _flash_cols(qs, k_ref, vt_ref, s_ref, acc_ref, dk, dv, q_tile, tq, tk, pos_of_col):
    cols = qs[0].shape[0]
    dve = dv + ONES_ROWS
    assert tq == 2 * tk
    heads = range(len(qs))
    q_chunk = pos_of_col(lax.broadcasted_iota(jnp.int32, (tk, cols), 1)) >> CHUNK_SHIFT
    k_row = lax.broadcasted_iota(jnp.int32, (tk, cols), 0)

    def diag_mask(d):
        return ((k_row + d * tk) >> CHUNK_SHIFT) <= q_chunk

    def scores(tile, slot, mask):
        off = pl.multiple_of(tile * tk, tk)
        col_max = []
        for g in heads:
            s = _dot_nt(k_ref[0, pl.ds(off, tk), g * dk:(g + 1) * dk], qs[g])
            if mask is not None:
                s = jnp.where(mask, s, NEG_INF)
            s_ref[slot, g] = s
            col_max.append(jnp.max(s, axis=0, keepdims=True))
        return tuple(col_max)

    def consume(tile, slot, col_max, ms):
        out = []
        for g in heads:
            m_new = jnp.maximum(ms[g], col_max[g])
            alpha = jnp.exp2(ms[g] - m_new)
            p = jnp.exp2(s_ref[slot, g] - m_new).astype(BF16)
            acc_ref[g] = alpha * acc_ref[g] + _dot(vt_ref[0, tile, g * dve:(g + 1) * dve, :], p)
            out.append(m_new)
        return tuple(out)

    def pair(i, carry, next_mask):
        cm0, ms = carry
        cm1 = scores(2 * i + 1, 1, None)
        ms = consume(2 * i, 0, cm0, ms)
        cm0 = scores(2 * i + 2, 0, next_mask)
        ms = consume(2 * i + 1, 1, cm1, ms)
        return cm0, ms

    acc_ref[...] = jnp.zeros(acc_ref.shape, F32)
    ms = (jnp.full((1, cols), NEG_INF, F32),) * len(qs)
    cm0 = scores(0, 0, jnp.logical_or(q_tile > 0, diag_mask(0)))
    carry = lax.fori_loop(0, jnp.maximum(q_tile - 1, 0), lambda i, c: pair(i, c, None), (cm0, ms))
    carry = lax.cond(q_tile > 0, lambda c: pair(q_tile - 1, c, diag_mask(0)), lambda c: c, carry)
    cm0, ms = carry
    cm1 = scores(2 * q_tile + 1, 1, diag_mask(1))
    ms = consume(2 * q_tile, 0, cm0, ms)
    consume(2 * q_tile + 1, 1, cm1, ms)
    return [(acc_ref[g, :dv, :], acc_ref[g, dv:dv + 1, :]) for g in heads]


def _lambda_full(lq1_ref, lk1_ref, lq2_ref, lk2_ref, lam_init):
    a = jnp.sum(lq1_ref[...] * lk1_ref[...], axis=-1, keepdims=True)
    b = jnp.sum(lq2_ref[...] * lk2_ref[...], axis=-1, keepdims=True)
    return jnp.exp(a) - jnp.exp(b) + lam_init


def _flash_diff_kernel(tq, tk, lam_init, q_ref, k_ref, vt_ref, lq1_ref, lk1_ref, lq2_ref, lk2_ref,
                       gsub_ref, o_ref, s_ref, acc_ref):
    heads = q_ref.shape[2] // DV_A
    qs = []
    for g in range(heads):
        q = q_ref[0, :, g * DV_A:(g + 1) * DV_A]
        lane = lax.broadcasted_iota(jnp.int32, q.shape, 1)
        zero = jnp.zeros_like(q)
        qs.append(jnp.concatenate(
            [jnp.where(lane < DQK, q, zero), jnp.where(lane >= DQK, q, zero)], axis=0))
    res = _flash_cols(qs, k_ref, vt_ref, s_ref, acc_ref, DV_A, DV_A, pl.program_id(2), tq, tk,
                      lambda r: jnp.where(r >= tq, r - tq, r))
    lam = _lambda_full(lq1_ref, lk1_ref, lq2_ref, lk2_ref, lam_init)
    for g, (acc, l) in enumerate(res):
        o = acc / l
        o = o[:, :tq] - lam * o[:, tq:]
        o = o * lax.rsqrt(jnp.mean(o * o, axis=0, keepdims=True) + SUBLN_EPS)
        o_ref[0, :, g * DV_A:(g + 1) * DV_A] = (
            o * gsub_ref[...] * (1.0 - lam_init)).T.astype(o_ref.dtype)


def _flash_mla_kernel(tq, tk, q_ref, k_ref, vt_ref, o_ref, s_ref, acc_ref):
    heads = q_ref.shape[2] // 256
    qs = [q_ref[0, :, g * 256:(g + 1) * 256] for g in range(heads)]
    res = _flash_cols(qs, k_ref, vt_ref, s_ref, acc_ref, 256, V_DIM, pl.program_id(2), tq, tk,
                      lambda r: r)
    for g, (acc, l) in enumerate(res):
        o_ref[0, :, g * V_DIM:(g + 1) * V_DIM] = (acc / l).T.astype(o_ref.dtype)


def _flash_diff(qa, k16, v16t, lam_refs, gsub_col, lam_init, b, s, tk):
    tq = 2 * tk
    heads = FLASH_DIFF_HEADS
    gw = heads * DV_A
    vec = pl.BlockSpec((1, DQK), lambda bi, hi, qi: (0, 0))
    return pl.pallas_call(
        functools.partial(_flash_diff_kernel, tq, tk, lam_init),
        grid=(b, H_A // heads, s // tq),
        in_specs=[
            pl.BlockSpec((1, tq, gw), lambda bi, hi, qi: (bi, qi, hi)),
            pl.BlockSpec((1, s, gw), lambda bi, hi, qi: (bi, 0, hi)),
            pl.BlockSpec((1, s // tk, heads * (DV_A + ONES_ROWS), tk), lambda bi, hi, qi: (bi, 0, hi, 0)),
            vec, vec, vec, vec,
            pl.BlockSpec((DV_A, 1), lambda bi, hi, qi: (0, 0)),
        ],
        out_specs=pl.BlockSpec((1, tq, gw), lambda bi, hi, qi: (bi, qi, hi)),
        out_shape=jax.ShapeDtypeStruct((b, s, W_A), BF16),
        scratch_shapes=[pltpu.VMEM((2, heads, tk, 2 * tq), F32), pltpu.VMEM((heads, DV_A + ONES_ROWS, 2 * tq), F32)],
        compiler_params=_params("arbitrary", "arbitrary", "arbitrary"),
        name="flash_diff",
    )(qa, k16, v16t, *lam_refs, gsub_col)


def _flash_mla(qcat, kcat, vmlat, b, s, tk):
    tq = 2 * tk
    g = FLASH_MLA_HEADS
    return pl.pallas_call(
        functools.partial(_flash_mla_kernel, tq, tk),
        grid=(b, H_B // g, s // tq),
        in_specs=[
            pl.BlockSpec((1, tq, g * 256), lambda bi, hi, qi: (bi, qi, hi)),
            pl.BlockSpec((1, s, g * 256), lambda bi, hi, qi: (bi, 0, hi)),
            pl.BlockSpec((1, s // tk, g * (V_DIM + ONES_ROWS), tk), lambda bi, hi, qi: (bi, 0, hi, 0)),
        ],
        out_specs=pl.BlockSpec((1, tq, g * V_DIM), lambda bi, hi, qi: (bi, qi, hi)),
        out_shape=jax.ShapeDtypeStruct((b, s, W_B), BF16),
        scratch_shapes=[pltpu.VMEM((2, g, tk, tq), F32), pltpu.VMEM((g, V_DIM + ONES_ROWS, tq), F32)],
        compiler_params=_params("arbitrary", "arbitrary", "arbitrary"),
        name="flash_mla",
    )(qcat, kcat, vmlat)


def _dec_update(s, v, m_ref, l_ref, acc_ref, g):
    m_new, l, acc = _online_softmax_step(s, v, m_ref[g], l_ref[g], acc_ref[g])
    m_ref[g] = m_new
    l_ref[g] = l
    acc_ref[g] = acc


def _dec_diff_kernel(n, lam_init, q_ref, k_ref, v_ref, kn_ref, vn_ref, lq1_ref, lk1_ref, lq2_ref,
                     lk2_ref, gsub_ref, o_ref, qbd_ref, m_ref, l_ref, acc_ref):
    j = pl.program_id(1)
    gw = 4 * DV_A
    chains = [(sq, g) for sq in range(DEC_DIFF_SEQS) for g in range(2)]

    @pl.when(j == 0)
    def _():
        for c, (sq, g) in enumerate(chains):
            qg = q_ref[sq * n:(sq + 1) * n, g * gw:(g + 1) * gw]
            lane = lax.broadcasted_iota(jnp.int32, qg.shape, 1)
            blocks = [jnp.where((lane >> DQK_SHIFT) == blk, qg, jnp.zeros_like(qg))
                      for blk in range(8)]
            qbd_ref[c] = jnp.concatenate(blocks, axis=0)
        m_ref[...] = jnp.full(m_ref.shape, NEG_INF, F32)
        l_ref[...] = jnp.zeros(l_ref.shape, F32)
        acc_ref[...] = jnp.zeros(acc_ref.shape, F32)

    def group_lanes(ref, g):
        keys = ref.shape[0] // H_A
        return jnp.concatenate(
            [ref[pl.ds(g * 4 + hl, keys, stride=H_A), :] for hl in range(4)], axis=1).astype(BF16)

    def attend(kview, vview):
        scores = [_dot_nt(qbd_ref[c], group_lanes(kview(sq), g)) for c, (sq, g) in enumerate(chains)]
        for c, (sq, g) in enumerate(chains):
            _dec_update(scores[c], group_lanes(vview(sq), g), m_ref, l_ref, acc_ref, c)

    attend(lambda sq: k_ref.at[sq], lambda sq: v_ref.at[sq])

    @pl.when(j == pl.num_programs(1) - 1)
    def _():
        lam = _lambda_full(lq1_ref, lk1_ref, lq2_ref, lk2_ref, lam_init)
        new = n * H_A
        attend(lambda sq: kn_ref.at[sq * new:(sq + 1) * new], lambda sq: vn_ref.at[sq * new:(sq + 1) * new])
        for c, (sq, g) in enumerate(chains):
            o = acc_ref[c] / l_ref[c]
            for hl in range(4):
                cols = slice(hl * DV_A, (hl + 1) * DV_A)
                o0 = o[(2 * hl) * n:(2 * hl + 1) * n, cols]
                o1 = o[(2 * hl + 1) * n:(2 * hl + 2) * n, cols]
                od = _rms(o0 - lam * o1, SUBLN_EPS) * gsub_ref[...] * (1.0 - lam_init)
                hd = g * 4 + hl
                o_ref[sq * n:(sq + 1) * n, hd * DV_A:(hd + 1) * DV_A] = od.astype(o_ref.dtype)


def _dec_diff(qa, cache_k, cache_v, k_new, v_new, lam_refs, gsub, lam_init, b, n):
    past = cache_k.shape[1] // H_A
    tk = DEC_DIFF_KEYS
    sq = DEC_DIFF_SEQS
    vec = pl.BlockSpec((1, DQK), lambda bi, j: (0, 0))
    rows = pl.BlockSpec((sq * n, W_A), lambda bi, j: (bi, 0))
    new_rows = pl.BlockSpec((sq * n * H_A, DV_A), lambda bi, j: (bi, 0))
    cache_tile = pl.BlockSpec((sq, tk * H_A, DV_A), lambda bi, j: (bi, j, 0))
    return pl.pallas_call(
        functools.partial(_dec_diff_kernel, n, lam_init),
        grid=(b // sq, past // tk),
        in_specs=[
            rows, cache_tile, cache_tile, new_rows, new_rows, vec, vec, vec, vec,
            pl.BlockSpec((1, DV_A), lambda bi, j: (0, 0)),
        ],
        out_specs=rows,
        out_shape=jax.ShapeDtypeStruct((b * n, W_A), BF16),
        scratch_shapes=[
            pltpu.VMEM((2 * sq, 8 * n, 4 * DV_A), BF16),
            pltpu.VMEM((2 * sq, 8 * n, 1), F32),
            pltpu.VMEM((2 * sq, 8 * n, 1), F32),
            pltpu.VMEM((2 * sq, 8 * n, 4 * DV_A), F32),
        ],
        compiler_params=_params("arbitrary", "arbitrary"),
        name="dec_diff",
    )(qa, cache_k, cache_v, k_new, v_new, *lam_refs, gsub)


def _dec_mla_kernel(n, qlat_ref, qpe_ref, ckv_ref, kpet_ref, ckvn_ref, kpen_ref, wuv_ref, o_ref,
                    ql_ref, qp_ref, m_ref, l_ref, acc_ref):
    j = pl.program_id(1)
    seqs = range(DEC_SEQS)

    @pl.when(j == 0)
    def _():
        for sq in seqs:
            for hd in range(H_B):
                rows = slice(sq * n, (sq + 1) * n)
                ql_ref[sq, hd * n:(hd + 1) * n, :] = qlat_ref[rows, hd * KV_LORA:(hd + 1) * KV_LORA]
                qp_ref[sq, hd * n:(hd + 1) * n, :] = qpe_ref[rows, hd * 128:(hd + 1) * 128]
        m_ref[...] = jnp.full(m_ref.shape, NEG_INF, F32)
        l_ref[...] = jnp.zeros(l_ref.shape, F32)
        acc_ref[...] = jnp.zeros(acc_ref.shape, F32)

    def attend(ckv_of, pe_scores_of):
        ckvs = [ckv_of(sq).astype(BF16) for sq in seqs]
        scores = [_dot_nt(ql_ref[sq], ckvs[sq]) + pe_scores_of(sq, qp_ref[sq, :, :ROPE_DIM]) for sq in seqs]
        for sq in seqs:
            _dec_update(scores[sq], ckvs[sq], m_ref, l_ref, acc_ref, sq)

    attend(lambda sq: ckv_ref[sq], lambda sq, qp: _dot(qp, kpet_ref[sq].astype(BF16)))

    @pl.when(j == pl.num_programs(1) - 1)
    def _():
        attend(lambda sq: ckvn_ref[sq * n:(sq + 1) * n, :],
               lambda sq, qp: _dot_nt(qp, kpen_ref[sq * n:(sq + 1) * n, :].astype(BF16)))
        for sq in seqs:
            o = (acc_ref[sq] / l_ref[sq]).astype(BF16)
            for hd in range(H_B):
                o_ref[sq * n:(sq + 1) * n, hd * V_DIM:(hd + 1) * V_DIM] = _dot(
                    o[hd * n:(hd + 1) * n], wuv_ref[hd]).astype(o_ref.dtype)


def _dec_mla(qlat, qpe, cache_ckv, cache_kpe_t, ckv_new, kpe_new, wuv_h, b, n):
    past = cache_ckv.shape[1]
    tk = DEC_MLA_KEYS
    sq = DEC_SEQS
    return pl.pallas_call(
        functools.partial(_dec_mla_kernel, n),
        grid=(b // sq, past // tk),
        in_specs=[
            pl.BlockSpec((sq * n, H_B * KV_LORA), lambda bi, j: (bi, 0)),
            pl.BlockSpec((sq * n, H_B * 128), lambda bi, j: (bi, 0)),
            pl.BlockSpec((sq, tk, KV_LORA), lambda bi, j: (bi, j, 0)),
            pl.BlockSpec((sq, ROPE_DIM, tk), lambda bi, j: (bi, 0, j)),
            pl.BlockSpec((sq * n, KV_LORA), lambda bi, j: (bi, 0)),
            pl.BlockSpec((sq * n, ROPE_DIM), lambda bi, j: (bi, 0)),
            pl.BlockSpec((H_B, KV_LORA, V_DIM), lambda bi, j: (0, 0, 0)),
        ],
        out_specs=pl.BlockSpec((sq * n, W_B), lambda bi, j: (bi, 0)),
        out_shape=jax.ShapeDtypeStruct((b * n, W_B), BF16),
        scratch_shapes=[
            pltpu.VMEM((sq, H_B * n, KV_LORA), BF16),
            pltpu.VMEM((sq, H_B * n, 128), BF16),
            pltpu.VMEM((sq, H_B * n, 1), F32),
            pltpu.VMEM((sq, H_B * n, 1), F32),
            pltpu.VMEM((sq, H_B * n, KV_LORA), F32),
        ],
        compiler_params=_params("arbitrary", "arbitrary"),
        name="dec_mla",
    )(qlat, qpe, cache_ckv, cache_kpe_t, ckv_new, kpe_new, wuv_h)


def _out_proj_kernel(oa_ref, ob_ref, x_ref, shift_ref, scale_ref, gate_ref, gn_ref, wgate_ref, w_ref,
                     o_ref):
    nb = x_ref.shape[0]
    for _, rs, ts in _token_chains(x_ref):
        sg = _silu(_dot_nt(_prenorm_tokens(x_ref, rs, gn_ref, scale_ref, shift_ref), wgate_ref[...]))
        oa = (oa_ref[ts, :].astype(F32) * sg[:, :W_A]).astype(BF16)
        ob = (ob_ref[ts, :].astype(F32) * sg[:, W_A:]).astype(BF16)
        y = _dot(oa, w_ref[:W_A, :]) + _dot(ob, w_ref[W_A:, :])
        o_ref[:, rs, :] = x_ref[:, rs, :] + gate_ref[...] * y.reshape(nb, -1, D_MODEL)


def _pool_kernel(tiles_per_seq, pos0, carry, x_ref, shift_ref, scale_ref, gate_ref, gn_ref, pre_ref,
                 win_ref, wgrp_ref, sp_ref, wout_ref, gf_ref, y_ref, st_ref, uext_ref):
    nb, rows, _ = x_ref.shape
    t = pl.program_id(0) % tiles_per_seq
    if carry:
        @pl.when(t == 0)
        def _():
            uext_ref[:, :POOL_PREFIX, :] = pre_ref[...]
    else:
        uext_ref[:, :POOL_PREFIX, :] = pre_ref[...]

    for r0, r1 in _row_chains(rows):
        n = r1 - r0
        x = x_ref[:, r0:r1, :]
        h = _prenorm(x, gn_ref[...], scale_ref[...], shift_ref[...])
        z = _dot(h.reshape(nb * n, D_MODEL).astype(BF16), win_ref[...])
        u = z[:, :W_C].reshape(nb, n, W_C)
        sgate = _silu(z[:, W_C:])
        uext_ref[:, POOL_PREFIX + r0:POOL_PREFIX + r1, :] = u
        pos = pos0 + t * rows + r0 + lax.broadcasted_iota(jnp.int32, (1, n, 1), 1)
        parts = []
        for g, w in enumerate(POOL_WINDOWS):
            cols = slice(g * POOL_GW, (g + 1) * POOL_GW)
            ws = u[:, :, cols]
            for k in range(1, w):
                ws = ws + uext_ref[:, POOL_PREFIX + r0 - k:POOL_PREFIX + r1 - k, cols]
            cnt = jnp.minimum(pos + 1, w).astype(F32)
            dgrp = (ws / cnt - u[:, :, cols]).reshape(nb * n, POOL_GW).astype(BF16)
            parts.append(_dot(dgrp, wgrp_ref[g]))
        m = jnp.concatenate(parts, axis=1) * sp_ref[...]
        y = _dot((m * sgate).astype(BF16), wout_ref[...])
        x2 = x + gate_ref[...] * y.reshape(nb, n, D_MODEL)
        y_ref[:, r0:r1, :] = _rms(x2, EPS) * gf_ref[...]
    tail = uext_ref[:, rows:rows + POOL_PREFIX, :]
    st_ref[...] = tail
    if carry:
        uext_ref[:, :POOL_PREFIX, :] = tail


def _rot_half_cols(w):
    half = ROPE_DIM // 2
    return jnp.concatenate([-w[..., half:], w[..., :half]], axis=-1)


def _rope_tables(start, count, repeat=1):
    half = ROPE_DIM // 2
    inv = ROPE_THETA ** (-np.arange(half, dtype=np.float64) / half)
    ang = np.arange(start, start + count, dtype=np.float64)[:, None] * inv[None, :]
    zero = np.zeros((count, 128 - ROPE_DIM))
    tabs = [np.tile(np.concatenate([t, t, zero], axis=1), (repeat, 1)) for t in (np.cos(ang), np.sin(ang))]
    return tuple(jnp.asarray(t, F32) for t in tabs)


def _group(x3, mod, cos, sin, sample, cache, w, tm_att, tm_pool):
    bt, st, _ = x3.shape
    tokens = bt * st
    lam_init = _lambda_init(0)

    if sample:
        nb, rows, tps = SAMPLE_TILE_SEQS, st, 1
        n_tiles = bt // nb
        x_spec = pl.BlockSpec((nb, rows, D_MODEL), lambda i: (i, 0, 0))

        def mod_spec(layer, comp):
            blk = (layer * 3 + comp) * MOD_ROWS // nb
            return pl.BlockSpec((nb, 1, D_MODEL), lambda i: (blk + i, 0, 0))
    else:
        nb, rows = 1, tm_att
        tps = st // rows
        n_tiles = bt * tps
        x_spec = pl.BlockSpec((1, rows, D_MODEL), lambda i: (i // tps, i % tps, 0))

        def mod_spec(layer, comp):
            base = (layer * 3 + comp) * MOD_ROWS + 16
            return pl.BlockSpec((1, 1, D_MODEL), lambda i: (base + i // tps, 0, 0))

    tm = nb * rows

    def tok(width):
        return pl.BlockSpec((tm, width), lambda i: (i, 0))

    def tok_shape(width, dt):
        return jax.ShapeDtypeStruct((tokens, width), dt)

    vec = lambda width: pl.BlockSpec((1, width), lambda i: (0, 0))
    if sample:
        tab = pl.BlockSpec((tm, 128), lambda i: (0, 0))
    else:
        tab = pl.BlockSpec((tm, 128), lambda i: (i % tps, 0))

    heads_spec = pl.BlockSpec((tm * H_A, DV_A), lambda i: (i, 0))
    heads_shape = jax.ShapeDtypeStruct((tokens * H_A, DV_A), F32)
    vt_rows = H_A * (DV_A + ONES_ROWS)
    key_tile = min(tm, CHAIN_ROWS)
    vt_spec = pl.BlockSpec((1, tm // key_tile, vt_rows, key_tile), lambda i: (i // tps, i % tps, 0, 0))
    vt_shape = jax.ShapeDtypeStruct((bt, st // key_tile, vt_rows, key_tile), BF16)
    qkv_out = pl.pallas_call(
        _in_qkv_kernel,
        grid=(n_tiles,),
        in_specs=[x_spec, mod_spec(0, 0), mod_spec(0, 1), vec(D_MODEL), _resident_rows(0, 3 * W_A, D_MODEL)],
        out_specs=[tok(W_A), heads_spec, heads_spec] + ([] if sample else [tok(W_A), vt_spec]),
        out_shape=[tok_shape(W_A, BF16), heads_shape, heads_shape]
        + ([] if sample else [tok_shape(W_A, BF16), vt_shape]),
        compiler_params=_params("arbitrary"),
        name="in_qkv",
    )(x3, mod, mod, w["g_norm0"], w["w_in_t"])
    qa, k32, v32 = qkv_out[:3]

    lat_in = [x_spec, mod_spec(0, 0), mod_spec(0, 1), vec(D_MODEL), tab, tab,
              _resident_rows(3 * W_A, Q_LORA + KV_LORA, D_MODEL), _resident((256, D_MODEL)),
              vec(Q_LORA), vec(KV_LORA),
              _resident((Q_LORA, H_B * 128)), _resident((Q_LORA, H_B * 128)), _resident((Q_LORA, H_B * 128))]
    lat_args = [x3, mod, mod, w["g_norm0"], cos, sin, w["w_in_t"], w["wkpe"], w["g_qa"], w["g_kva"],
                w["wuq"], w["wuqr"], w["wuqrr"]]
    lam_refs = (w["lam_q1"], w["lam_k1"], w["lam_q2"], w["lam_k2"])

    if sample:
        ckv32, kpe32, qlat, qpe = pl.pallas_call(
            _in_lat_sample_kernel,
            grid=(n_tiles,),
            in_specs=lat_in + [_resident((H_B, 128, KV_LORA))],
            out_specs=[tok(KV_LORA), tok(ROPE_DIM), tok(H_B * KV_LORA), tok(H_B * 128)],
            out_shape=[tok_shape(KV_LORA, F32), tok_shape(ROPE_DIM, F32),
                       tok_shape(H_B * KV_LORA, BF16), tok_shape(H_B * 128, BF16)],
            compiler_params=_params("arbitrary"),
            name="in_lat_sample",
        )(*lat_args, w["wukt"])
        cache_k, cache_v, cache_ckv, cache_kpe = cache
        o_a = _dec_diff(qa, cache_k, cache_v, k32, v32, lam_refs, w["g_subln"], lam_init, bt, st)
        o_b = _dec_mla(qlat, qpe, cache_ckv, cache_kpe, ckv32, kpe32, w["wuv_h"], bt, st)
    else:
        ckv32, kpe32, qcat, kcat, vmla = pl.pallas_call(
            _in_lat_prompt_kernel,
            grid=(n_tiles,),
            in_specs=lat_in + [_resident((KV_LORA, H_B * 128)), _resident((KV_LORA, H_B * 128))],
            out_specs=[tok(KV_LORA), tok(ROPE_DIM), tok(H_B * 256), tok(H_B * 256), vt_spec],
            out_shape=[tok_shape(KV_LORA, F32), tok_shape(ROPE_DIM, F32),
                       tok_shape(H_B * 256, BF16), tok_shape(H_B * 256, BF16), vt_shape],
            compiler_params=_params("arbitrary"),
            name="in_lat_prompt",
        )(*lat_args, w["wuk"], w["wuv"])
        k16, v16t = qkv_out[3:]
        o_a = _flash_diff(qa.reshape(bt, st, W_A), k16.reshape(bt, st, W_A), v16t, lam_refs,
                          w["g_subln"].reshape(DV_A, 1), lam_init, bt, st, key_tile).reshape(tokens, W_A)
        o_b = _flash_mla(qcat.reshape(bt, st, H_B * 256), kcat.reshape(bt, st, H_B * 256),
                         vmla, bt, st, key_tile).reshape(tokens, W_B)

    x1 = pl.pallas_call(
        _out_proj_kernel,
        grid=(n_tiles,),
        in_specs=[tok(W_A), tok(W_B), x_spec, mod_spec(0, 0), mod_spec(0, 1), mod_spec(0, 2), vec(D_MODEL),
                  _resident_rows(4 * W_A + ROPE_DIM, W_ATT, D_MODEL), _resident((W_ATT, D_MODEL))],
        out_specs=x_spec,
        out_shape=jax.ShapeDtypeStruct(x3.shape, F32),
        compiler_params=_params("arbitrary"),
        name="out_proj",
    )(o_a, o_b, x3, mod, mod, mod, w["g_norm0"], w["w_in_t"], w["wout_att"])

    if sample:
        p_nb, p_rows, p_tiles, p_tps = nb, st, n_tiles, 1
        px_spec = x_spec
        pmod = mod_spec
        pre_spec = pl.BlockSpec((nb, POOL_PREFIX, W_C), lambda i: (i, 0, 0))
        st_spec = pl.BlockSpec((nb, POOL_PREFIX, W_C), lambda i: (i, 0, 0))
        pos0 = cache[2].shape[1]
        prefix = w["pool_prefix"]
    else:
        p_nb, p_rows = 1, tm_pool
        p_tps = st // p_rows
        p_tiles = bt * p_tps
        px_spec = pl.BlockSpec((1, p_rows, D_MODEL), lambda i: (i // p_tps, i % p_tps, 0))

        def pmod(layer, comp):
            base = (layer * 3 + comp) * MOD_ROWS + 16
            return pl.BlockSpec((1, 1, D_MODEL), lambda i: (base + i // p_tps, 0, 0))

        pre_spec = pl.BlockSpec((1, POOL_PREFIX, W_C), lambda i: (0, 0, 0))
        st_spec = pl.BlockSpec((1, POOL_PREFIX, W_C), lambda i: (i // p_tps, 0, 0))
        pos0 = 0
        prefix = jnp.zeros((1, POOL_PREFIX, W_C), F32)

    y, state = pl.pallas_call(
        functools.partial(_pool_kernel, p_tps, pos0, not sample),
        grid=(p_tiles,),
        in_specs=[px_spec, pmod(1, 0), pmod(1, 1), pmod(1, 2), vec(D_MODEL), pre_spec,
                  _resident((D_MODEL, 2 * W_C)), _resident((N_POOL_GROUPS, POOL_GW, POOL_GW)),
                  vec(W_C), _resident((W_C, D_MODEL)), vec(D_MODEL)],
        out_specs=[px_spec, st_spec],
        out_shape=[jax.ShapeDtypeStruct(x3.shape, F32),
                   jax.ShapeDtypeStruct((bt, POOL_PREFIX, W_C), F32)],
        scratch_shapes=[pltpu.VMEM((p_nb, POOL_PREFIX + p_rows, W_C), F32)],
        compiler_params=_params("arbitrary"),
        name="pool",
    )(x1, mod, mod, mod, w["g_norm1"], prefix, w["win_pool"], w["wgrp"], w["scale_pool"],
      w["wout_pool"], w["g_final"])

    return (y,
            k32.reshape(1, bt, st, H_A, DV_A), v32.reshape(1, bt, st, H_A, DV_A),
            ckv32.reshape(1, bt, st, KV_LORA), kpe32.reshape(1, bt, st, ROPE_DIM),
            state[None, :, 1:, :])


def kernel(x_prompt, x_sample, c_prompt, c_sample, cache_diff_k, cache_diff_v, cache_mla_ckv, cache_mla_kpe, state_pool, g_norm, w_ada, b_ada, w_in_att, w_out_att, lam_q1, lam_k1, lam_q2, lam_k2, g_subln, g_qa, w_uq, w_uqr, w_uk, w_uv, g_kva, w_in_pool, w_grp_pool, scale_pool, w_out_pool, g_final):
    b_p, s_p, _ = x_prompt.shape
    b_s, n_s, _ = x_sample.shape
    past = cache_diff_k.shape[2]

    c_all = jnp.concatenate(
        [c_sample, c_prompt, jnp.zeros((MOD_ROWS - b_s - b_p, D_MODEL), F32)], axis=0)
    mod = _ada(c_all, w_ada, b_ada)

    w_in_t = w_in_att[0].T
    kpe_w = w_in_t[4 * W_A:4 * W_A + ROPE_DIM].T
    zpad = jnp.zeros((D_MODEL, 128 - ROPE_DIM), F32)
    uqr = w_uqr[0]
    pad_heads = lambda a: jnp.pad(a, ((0, 0), (0, 0), (0, 128 - ROPE_DIM))).reshape(Q_LORA, H_B * 128)
    w = {
        "g_norm0": g_norm[0:1], "g_norm1": g_norm[1:2],
        "w_in_t": w_in_t.astype(BF16),
        "wkpe": jnp.concatenate([kpe_w, zpad, _rot_half_cols(kpe_w), zpad], axis=1).T.astype(BF16),
        "g_qa": g_qa, "g_kva": g_kva,
        "wuq": w_uq[0].reshape(Q_LORA, H_B * NOPE_DIM).astype(BF16),
        "wuqr": pad_heads(uqr).astype(BF16),
        "wuqrr": pad_heads(_rot_half_cols(uqr)).astype(BF16),
        "wuk": w_uk[0].reshape(KV_LORA, H_B * NOPE_DIM).astype(BF16),
        "wuv": w_uv[0].reshape(KV_LORA, H_B * V_DIM).astype(BF16),
        "wukt": jnp.transpose(w_uk[0], (1, 2, 0)).astype(BF16),
        "wuv_h": jnp.transpose(w_uv[0], (1, 0, 2)).astype(BF16),
        "wout_att": w_out_att[0].astype(BF16),
        "lam_q1": lam_q1, "lam_k1": lam_k1, "lam_q2": lam_q2, "lam_k2": lam_k2,
        "g_subln": g_subln,
        "win_pool": w_in_pool[0].astype(BF16),
        "wgrp": w_grp_pool[0].astype(BF16),
        "scale_pool": scale_pool,
        "wout_pool": w_out_pool[0].astype(BF16),
        "g_final": g_final.reshape(1, D_MODEL),
        "pool_prefix": jnp.pad(state_pool[0], ((0, 0), (1, 0), (0, 0))),
    }

    cos_p, sin_p = _rope_tables(0, s_p)
    cos_s, sin_s = _rope_tables(past, n_s, SAMPLE_TILE_SEQS)

    out_p = _group(x_prompt, mod, cos_p, sin_p, False, None, w, 2 * CHAIN_ROWS, 2 * CHAIN_ROWS)
    cache = (cache_diff_k[0].reshape(b_s, past * H_A, DV_A), cache_diff_v[0].reshape(b_s, past * H_A, DV_A),
             cache_mla_ckv[0], jnp.swapaxes(cache_mla_kpe[0], 1, 2))
    out_s = _group(x_sample, mod, cos_s, sin_s, True, cache, w, 0, 0)
    return (out_p[0], out_s[0]) + out_p[1:] + out_s[1:]
```

```python
import functools
import math

import jax
import jax.numpy as jnp
import numpy as np
from jax import lax
from jax.experimental import pallas as pl
from jax.experimental.pallas import tpu as pltpu

F32 = jnp.float32
BF16 = jnp.bfloat16

D_MODEL = 2048
CHUNK = 64
CHUNK_SHIFT = 6
H_A = 8
DQK = 64
DQK_SHIFT = 6
DV_A = 2 * DQK
W_A = H_A * DV_A
H_B = 8
Q_LORA = 512
KV_LORA = 512
NOPE_DIM = 128
ROPE_DIM = 64
V_DIM = 128
W_B = H_B * V_DIM
ROPE_THETA = 10000.0
W_ATT = W_A + W_B
POOL_WINDOWS = (2, 4, 8, 16)
N_POOL_GROUPS = 4
W_C = D_MODEL
POOL_GW = W_C // N_POOL_GROUPS
POOL_STATE = max(POOL_WINDOWS) - 1
POOL_PREFIX = POOL_STATE + 1
EPS = 1e-6
SUBLN_EPS = 1e-5
NEG_INF = -1e30
LOG2E = math.log2(math.e)
DIFF_QSCALE = DQK ** -0.5 * LOG2E
MLA_QSCALE = (NOPE_DIM + ROPE_DIM) ** -0.5 * LOG2E
MOD_ROWS = 32
FLASH_DIFF_HEADS = 4
FLASH_MLA_HEADS = 4
DEC_SEQS = 2
DEC_MLA_KEYS = 2048
DEC_DIFF_SEQS = 1
DEC_DIFF_KEYS = 2048
CHAIN_ROWS = 256
ONES_ROWS = 16
SAMPLE_TILE_SEQS = 8
VMEM_LIMIT = 56 * 1024 * 1024


def _lambda_init(layer_idx):
    return 0.8 - 0.6 * math.exp(-0.3 * layer_idx)


def _rms(x, eps):
    return x * lax.rsqrt(jnp.mean(x * x, axis=-1, keepdims=True) + eps)


def _silu(x):
    return x * (1.0 / (1.0 + jnp.exp(-x)))


def _dot(a, b):
    return jnp.dot(a, b, preferred_element_type=F32)


def _dot_nt(a, b):
    return lax.dot_general(a, b, (((1,), (1,)), ((), ())), preferred_element_type=F32)


def _resident(shape):
    nd = len(shape)
    return pl.BlockSpec(shape, lambda *_: (0,) * nd, pipeline_mode=pl.Buffered(1))


def _resident_rows(row0, rows, cols):
    return pl.BlockSpec((pl.Element(rows), pl.Element(cols)), lambda *_: (row0, 0),
                        pipeline_mode=pl.Buffered(1))


def _params(*sem):
    return pltpu.CompilerParams(dimension_semantics=sem, vmem_limit_bytes=VMEM_LIMIT)


def _ada_kernel(c_ref, w_ref, b_ref, o_ref):
    cond = _silu(c_ref[...]).astype(BF16)
    w = w_ref[0].astype(BF16)
    o_ref[:, 0, :] = _dot(cond, w) + b_ref[0, 0]


def _ada(c_all, w_ada, b_ada):
    depth = w_ada.shape[0]
    tn = 1024
    per = D_MODEL // tn
    return pl.pallas_call(
        _ada_kernel,
        grid=(depth, 3 * per),
        in_specs=[
            pl.BlockSpec((MOD_ROWS, D_MODEL), lambda l, j: (0, 0)),
            pl.BlockSpec((1, D_MODEL, tn), lambda l, j: (l, 0, j)),
            pl.BlockSpec((1, 1, 1, tn), lambda l, j: (l, j // per, 0, j % per)),
        ],
        out_specs=pl.BlockSpec((MOD_ROWS, 1, tn), lambda l, j: (l * 3 + j // per, 0, j % per)),
        out_shape=jax.ShapeDtypeStruct((depth * 3 * MOD_ROWS, 1, D_MODEL), F32),
        compiler_params=_params("arbitrary", "arbitrary"),
        name="ada",
    )(c_all, w_ada, b_ada.reshape(depth, 3, 1, D_MODEL))


def _row_chains(rows):
    step = min(rows, CHAIN_ROWS)
    return [(r, r + step) for r in range(0, rows, step)]


def _prenorm(x, gn, scale, shift):
    return _rms(x, EPS) * gn * (1.0 + scale) + shift


def _store_vt(vt_ref, tile, v, heads, dv):
    vt = v.T.astype(BF16)
    ones = jnp.ones((ONES_ROWS, v.shape[0]), BF16)
    for hd in range(heads):
        base = hd * (dv + ONES_ROWS)
        vt_ref[0, tile, base:base + dv, :] = vt[hd * dv:(hd + 1) * dv]
        vt_ref[0, tile, base + dv:base + dv + ONES_ROWS, :] = ones


def _token_chains(x_ref):
    nb, rows, _ = x_ref.shape
    chains = _row_chains(rows)
    assert nb == 1 or len(chains) == 1
    return [(c, slice(r0, r1), slice(nb * r0, nb * r1)) for c, (r0, r1) in enumerate(chains)]


def _prenorm_tokens(x_ref, rs, gn_ref, scale_ref, shift_ref):
    x = x_ref[:, rs, :]
    h = _prenorm(x, gn_ref[...], scale_ref[...], shift_ref[...])
    return h.reshape(x.shape[0] * x.shape[1], D_MODEL).astype(BF16)


def _in_qkv_kernel(x_ref, shift_ref, scale_ref, gn_ref, wqkv_ref, qa_ref, k32_ref, v32_ref, *flash_refs):
    for c, rs, ts in _token_chains(x_ref):
        z = _dot_nt(_prenorm_tokens(x_ref, rs, gn_ref, scale_ref, shift_ref), wqkv_ref[...])
        qa_ref[ts, :] = (z[:, :W_A] * DIFF_QSCALE).astype(BF16)
        ka = z[:, W_A:2 * W_A]
        va = z[:, 2 * W_A:]
        for hd in range(H_A):
            head_rows = pl.ds(ts.start * H_A + hd, ts.stop - ts.start, stride=H_A)
            k32_ref[head_rows, :] = ka[:, hd * DV_A:(hd + 1) * DV_A]
            v32_ref[head_rows, :] = va[:, hd * DV_A:(hd + 1) * DV_A]
        if flash_refs:
            k16_ref, vt_ref = flash_refs
            k16_ref[ts, :] = ka.astype(BF16)
            _store_vt(vt_ref, c, va, H_A, DV_A)


def _latent_chain(x_ref, rs, ts, shift_ref, scale_ref, gn_ref, cos_ref, sin_ref, wc_ref, wkpe_ref,
                  gqa_ref, gkva_ref, wuq_ref, wuqr_ref, wuqrr_ref, ckv32_ref, kpe32_ref):
    hb = _prenorm_tokens(x_ref, rs, gn_ref, scale_ref, shift_ref)
    zc = _dot_nt(hb, wc_ref[...])
    cq = (_rms(zc[:, :Q_LORA], EPS) * gqa_ref[...]).astype(BF16)
    ckv = _rms(zc[:, Q_LORA:], EPS) * gkva_ref[...]
    ckv32_ref[ts, :] = ckv
    cos = cos_ref[ts, :]
    sin = sin_ref[ts, :]
    zk = _dot_nt(hb, wkpe_ref[...])
    kpe_pad = zk[:, :128] * cos + zk[:, 128:] * sin
    kpe32_ref[ts, :] = kpe_pad[:, :ROPE_DIM]
    q_nope = _dot(cq, wuq_ref[...]) * MLA_QSCALE
    cos8 = jnp.tile(cos, (1, H_B))
    sin8 = jnp.tile(sin, (1, H_B))
    q_pe = (_dot(cq, wuqr_ref[...]) * cos8 + _dot(cq, wuqrr_ref[...]) * sin8) * MLA_QSCALE
    return ckv, kpe_pad, q_nope, q_pe


def _in_lat_prompt_kernel(x_ref, shift_ref, scale_ref, gn_ref, cos_ref, sin_ref, wc_ref, wkpe_ref,
                          gqa_ref, gkva_ref, wuq_ref, wuqr_ref, wuqrr_ref, wuk_ref, wuv_ref,
                          ckv32_ref, kpe32_ref, qcat_ref, kcat_ref, vmla_ref):
    for c, rs, ts in _token_chains(x_ref):
        ckv, kpe_pad, q_nope, q_pe = _latent_chain(
            x_ref, rs, ts, shift_ref, scale_ref, gn_ref, cos_ref, sin_ref, wc_ref, wkpe_ref,
            gqa_ref, gkva_ref, wuq_ref, wuqr_ref, wuqrr_ref, ckv32_ref, kpe32_ref)
        ckvb = ckv.astype(BF16)
        k_nope = _dot(ckvb, wuk_ref[...])
        _store_vt(vmla_ref, c, _dot(ckvb, wuv_ref[...]), H_B, V_DIM)
        kpe_b = kpe_pad.astype(BF16)
        for hd in range(H_B):
            sl = slice(hd * 128, (hd + 1) * 128)
            qcat_ref[ts, hd * 256:hd * 256 + 128] = q_nope[:, sl].astype(BF16)
            qcat_ref[ts, hd * 256 + 128:(hd + 1) * 256] = q_pe[:, sl].astype(BF16)
            kcat_ref[ts, hd * 256:hd * 256 + 128] = k_nope[:, sl].astype(BF16)
            kcat_ref[ts, hd * 256 + 128:(hd + 1) * 256] = kpe_b


def _in_lat_sample_kernel(x_ref, shift_ref, scale_ref, gn_ref, cos_ref, sin_ref, wc_ref, wkpe_ref,
                          gqa_ref, gkva_ref, wuq_ref, wuqr_ref, wuqrr_ref, wukt_ref,
                          ckv32_ref, kpe32_ref, qlat_ref, qpe_ref):
    for _, rs, ts in _token_chains(x_ref):
        _, _, q_nope, q_pe = _latent_chain(
            x_ref, rs, ts, shift_ref, scale_ref, gn_ref, cos_ref, sin_ref, wc_ref, wkpe_ref,
            gqa_ref, gkva_ref, wuq_ref, wuqr_ref, wuqrr_ref, ckv32_ref, kpe32_ref)
        qn = q_nope.astype(BF16)
        for hd in range(H_B):
            qlat_ref[ts, hd * KV_LORA:(hd + 1) * KV_LORA] = _dot(
                qn[:, hd * 128:(hd + 1) * 128], wukt_ref[hd]).astype(BF16)
        qpe_ref[ts, :] = q_pe.astype(BF16)


def _online_softmax_step(s, v, m, l, acc):
    m_new = jnp.maximum(m, jnp.max(s, axis=1, keepdims=True))
    alpha = jnp.exp2(m - m_new)
    p = jnp.exp2(s - m_new)
    l = alpha * l + jnp.sum(p, axis=1, keepdims=True)
    acc = alpha * acc + _dot(p.astype(BF16), v)
    return m_new, l, acc


def _flash_cols(qs, k_ref, vt_ref, s_ref, acc_ref, dk, dv, q_tile, tq, tk, pos_of_col):
    cols = qs[0].shape[0]
    dve = dv + ONES_ROWS
    late_cols = cols // 2
    assert tq == 2 * tk
    heads = range(len(qs))
    q_chunk = pos_of_col(lax.broadcasted_iota(jnp.int32, (tk, cols), 1)) >> CHUNK_SHIFT
    k_row = lax.broadcasted_iota(jnp.int32, (tk, cols), 0)

    def diag_mask(d):
        return ((k_row + d * tk) >> CHUNK_SHIFT) <= q_chunk

    def scores(tile, slot, mask, c0=0):
        off = pl.multiple_of(tile * tk, tk)
        col_max = []
        for g in heads:
            s = _dot_nt(k_ref[0, pl.ds(off, tk), g * dk:(g + 1) * dk], qs[g][c0:])
            if mask is not None:
                s = jnp.where(mask, s, NEG_INF)
            s_ref[slot, g, :, c0:] = s
            col_max.append(jnp.max(s, axis=0, keepdims=True))
        return tuple(col_max)

    def consume(tile, slot, col_max, ms, c0=0):
        out = []
        for g in heads:
            m_old = ms[g][:, c0:]
            m_new = jnp.maximum(m_old, col_max[g])
            alpha = jnp.exp2(m_old - m_new)
            p = jnp.exp2(s_ref[slot, g, :, c0:] - m_new).astype(BF16)
            acc_ref[g, :, c0:] = alpha * acc_ref[g, :, c0:] + _dot(
                vt_ref[0, tile, g * dve:(g + 1) * dve, :], p)
            out.append(m_new if c0 == 0 else jnp.concatenate([ms[g][:, :c0], m_new], axis=1))
        return tuple(out)

    def pair(i, carry, next_mask):
        cm0, ms = carry
        cm1 = scores(2 * i + 1, 1, None)
        ms = consume(2 * i, 0, cm0, ms)
        cm0 = scores(2 * i + 2, 0, next_mask)
        ms = consume(2 * i + 1, 1, cm1, ms)
        return cm0, ms

    acc_ref[...] = jnp.zeros(acc_ref.shape, F32)
    ms = (jnp.full((1, cols), NEG_INF, F32),) * len(qs)
    cm0 = scores(0, 0, jnp.logical_or(q_tile > 0, diag_mask(0)))
    carry = lax.fori_loop(0, jnp.maximum(q_tile - 1, 0), lambda i, c: pair(i, c, None), (cm0, ms))
    carry = lax.cond(q_tile > 0, lambda c: pair(q_tile - 1, c, diag_mask(0)), lambda c: c, carry)
    cm0, ms = carry
    cm1 = scores(2 * q_tile + 1, 1, diag_mask(1)[:, late_cols:], late_cols)
    ms = consume(2 * q_tile, 0, cm0, ms)
    consume(2 * q_tile + 1, 1, cm1, ms, late_cols)
    return [(acc_ref[g, :dv, :], acc_ref[g, dv:dv + 1, :]) for g in heads]


def _lambda_full(lq1_ref, lk1_ref, lq2_ref, lk2_ref, lam_init):
    a = jnp.sum(lq1_ref[...] * lk1_ref[...], axis=-1, keepdims=True)
    b = jnp.sum(lq2_ref[...] * lk2_ref[...], axis=-1, keepdims=True)
    return jnp.exp(a) - jnp.exp(b) + lam_init


def _flash_diff_kernel(tq, tk, lam_init, q_ref, k_ref, vt_ref, lq1_ref, lk1_ref, lq2_ref, lk2_ref,
                       gsub_ref, o_ref, s_ref, acc_ref):
    heads = q_ref.shape[2] // DV_A
    qs = []
    for g in range(heads):
        q = q_ref[0, :, g * DV_A:(g + 1) * DV_A]
        lane = lax.broadcasted_iota(jnp.int32, q.shape, 1)
        zero = jnp.zeros_like(q)
        qa = jnp.where(lane < DQK, q, zero)
        qb = jnp.where(lane >= DQK, q, zero)
        qs.append(jnp.concatenate([qa[:tk], qb[:tk], qa[tk:], qb[tk:]], axis=0))
    res = _flash_cols(qs, k_ref, vt_ref, s_ref, acc_ref, DV_A, DV_A, pl.program_id(2), tq, tk,
                      lambda r: (r & (tk - 1)) + jnp.where(r >= 2 * tk, tk, 0))
    lam = _lambda_full(lq1_ref, lk1_ref, lq2_ref, lk2_ref, lam_init)
    for g, (acc, l) in enumerate(res):
        o = acc / l
        o = (jnp.concatenate([o[:, :tk], o[:, 2 * tk:3 * tk]], axis=1)
             - lam * jnp.concatenate([o[:, tk:2 * tk], o[:, 3 * tk:]], axis=1))
        o = o * lax.rsqrt(jnp.mean(o * o, axis=0, keepdims=True) + SUBLN_EPS)
        o_ref[0, :, g * DV_A:(g + 1) * DV_A] = (
            o * gsub_ref[...] * (1.0 - lam_init)).T.astype(o_ref.dtype)


def _flash_mla_kernel(tq, tk, q_ref, k_ref, vt_ref, o_ref, s_ref, acc_ref):
    heads = q_ref.shape[2] // 256
    qs = [q_ref[0, :, g * 256:(g + 1) * 256] for g in range(heads)]
    res = _flash_cols(qs, k_ref, vt_ref, s_ref, acc_ref, 256, V_DIM, pl.program_id(2), tq, tk,
                      lambda r: r)
    for g, (acc, l) in enumerate(res):
        o_ref[0, :, g * V_DIM:(g + 1) * V_DIM] = (acc / l).T.astype(o_ref.dtype)


def _flash_diff(qa, k16, v16t, lam_refs, gsub_col, lam_init, b, s, tk):
    tq = 2 * tk
    heads = FLASH_DIFF_HEADS
    gw = heads * DV_A
    vec = pl.BlockSpec((1, DQK), lambda bi, hi, qi: (0, 0))
    return pl.pallas_call(
        functools.partial(_flash_diff_kernel, tq, tk, lam_init),
        grid=(b, H_A // heads, s // tq),
        in_specs=[
            pl.BlockSpec((1, tq, gw), lambda bi, hi, qi: (bi, qi, hi)),
            pl.BlockSpec((1, s, gw), lambda bi, hi, qi: (bi, 0, hi)),
            pl.BlockSpec((1, s // tk, heads * (DV_A + ONES_ROWS), tk), lambda bi, hi, qi: (bi, 0, hi, 0)),
            vec, vec, vec, vec,
            pl.BlockSpec((DV_A, 1), lambda bi, hi, qi: (0, 0)),
        ],
        out_specs=pl.BlockSpec((1, tq, gw), lambda bi, hi, qi: (bi, qi, hi)),
        out_shape=jax.ShapeDtypeStruct((b, s, W_A), BF16),
        scratch_shapes=[pltpu.VMEM((2, heads, tk, 2 * tq), F32), pltpu.VMEM((heads, DV_A + ONES_ROWS, 2 * tq), F32)],
        compiler_params=_params("arbitrary", "arbitrary", "arbitrary"),
        name="flash_diff",
    )(qa, k16, v16t, *lam_refs, gsub_col)


def _flash_mla(qcat, kcat, vmlat, b, s, tk):
    tq = 2 * tk
    g = FLASH_MLA_HEADS
    return pl.pallas_call(
        functools.partial(_flash_mla_kernel, tq, tk),
        grid=(b, H_B // g, s // tq),
        in_specs=[
            pl.BlockSpec((1, tq, g * 256), lambda bi, hi, qi: (bi, qi, hi)),
            pl.BlockSpec((1, s, g * 256), lambda bi, hi, qi: (bi, 0, hi)),
            pl.BlockSpec((1, s // tk, g * (V_DIM + ONES_ROWS), tk), lambda bi, hi, qi: (bi, 0, hi, 0)),
        ],
        out_specs=pl.BlockSpec((1, tq, g * V_DIM), lambda bi, hi, qi: (bi, qi, hi)),
        out_shape=jax.ShapeDtypeStruct((b, s, W_B), BF16),
        scratch_shapes=[pltpu.VMEM((2, g, tk, tq), F32), pltpu.VMEM((g, V_DIM + ONES_ROWS, tq), F32)],
        compiler_params=_params("arbitrary", "arbitrary", "arbitrary"),
        name="flash_mla",
    )(qcat, kcat, vmlat)


def _dec_update(s, v, m_ref, l_ref, acc_ref, g):
    m_new, l, acc = _online_softmax_step(s, v, m_ref[g], l_ref[g], acc_ref[g])
    m_ref[g] = m_new
    l_ref[g] = l
    acc_ref[g] = acc


def _dec_diff_kernel(n, lam_init, q_ref, k_ref, v_ref, kn_ref, vn_ref, lq1_ref, lk1_ref, lq2_ref,
                     lk2_ref, gsub_ref, o_ref, qbd_ref, m_ref, l_ref, acc_ref):
    j = pl.program_id(1)
    gw = 4 * DV_A
    chains = [(sq, g) for sq in range(DEC_DIFF_SEQS) for g in range(2)]

    @pl.when(j == 0)
    def _():
        for c, (sq, g) in enumerate(chains):
            qg = q_ref[sq * n:(sq + 1) * n, g * gw:(g + 1) * gw]
            lane = lax.broadcasted_iota(jnp.int32, qg.shape, 1)
            blocks = [jnp.where((lane >> DQK_SHIFT) == blk, qg, jnp.zeros_like(qg))
                      for blk in range(8)]
            qbd_ref[c] = jnp.concatenate(blocks, axis=0)
        m_ref[...] = jnp.full(m_ref.shape, NEG_INF, F32)
        l_ref[...] = jnp.zeros(l_ref.shape, F32)
        acc_ref[...] = jnp.zeros(acc_ref.shape, F32)

    def group_lanes(ref, g):
        keys = ref.shape[0] // H_A
        return jnp.concatenate(
            [ref[pl.ds(g * 4 + hl, keys, stride=H_A), :] for hl in range(4)], axis=1).astype(BF16)

    def attend(kview, vview):
        scores = [_dot_nt(qbd_ref[c], group_lanes(kview(sq), g)) for c, (sq, g) in enumerate(chains)]
        for c, (sq, g) in enumerate(chains):
            _dec_update(scores[c], group_lanes(vview(sq), g), m_ref, l_ref, acc_ref, c)

    attend(lambda sq: k_ref.at[sq], lambda sq: v_ref.at[sq])

    @pl.when(j == pl.num_programs(1) - 1)
    def _():
        lam = _lambda_full(lq1_ref, lk1_ref, lq2_ref, lk2_ref, lam_init)
        new = n * H_A
        attend(lambda sq: kn_ref.at[sq * new:(sq + 1) * new], lambda sq: vn_ref.at[sq * new:(sq + 1) * new])
        for c, (sq, g) in enumerate(chains):
            o = acc_ref[c] / l_ref[c]
            for hl in range(4):
                cols = slice(hl * DV_A, (hl + 1) * DV_A)
                o0 = o[(2 * hl) * n:(2 * hl + 1) * n, cols]
                o1 = o[(2 * hl + 1) * n:(2 * hl + 2) * n, cols]
                od = _rms(o0 - lam * o1, SUBLN_EPS) * gsub_ref[...] * (1.0 - lam_init)
                hd = g * 4 + hl
                o_ref[sq * n:(sq + 1) * n, hd * DV_A:(hd + 1) * DV_A] = od.astype(o_ref.dtype)


def _dec_diff(qa, cache_k, cache_v, k_new, v_new, lam_refs, gsub, lam_init, b, n):
    past = cache_k.shape[1] // H_A
    tk = DEC_DIFF_KEYS
    sq = DEC_DIFF_SEQS
    vec = pl.BlockSpec((1, DQK), lambda bi, j: (0, 0))
    rows = pl.BlockSpec((sq * n, W_A), lambda bi, j: (bi, 0))
    new_rows = pl.BlockSpec((sq * n * H_A, DV_A), lambda bi, j: (bi, 0))
    cache_tile = pl.BlockSpec((sq, tk * H_A, DV_A), lambda bi, j: (bi, j, 0))
    return pl.pallas_call(
        functools.partial(_dec_diff_kernel, n, lam_init),
        grid=(b // sq, past // tk),
        in_specs=[
            rows, cache_tile, cache_tile, new_rows, new_rows, vec, vec, vec, vec,
            pl.BlockSpec((1, DV_A), lambda bi, j: (0, 0)),
        ],
        out_specs=rows,
        out_shape=jax.ShapeDtypeStruct((b * n, W_A), BF16),
        scratch_shapes=[
            pltpu.VMEM((2 * sq, 8 * n, 4 * DV_A), BF16),
            pltpu.VMEM((2 * sq, 8 * n, 1), F32),
            pltpu.VMEM((2 * sq, 8 * n, 1), F32),
            pltpu.VMEM((2 * sq, 8 * n, 4 * DV_A), F32),
        ],
        compiler_params=_params("arbitrary", "arbitrary"),
        name="dec_diff",
    )(qa, cache_k, cache_v, k_new, v_new, *lam_refs, gsub)


def _dec_mla_kernel(n, qlat_ref, qpe_ref, ckv_ref, kpet_ref, ckvn_ref, kpen_ref, wuv_ref, o_ref,
                    ql_ref, qp_ref, m_ref, l_ref, acc_ref):
    j = pl.program_id(1)
    seqs = range(DEC_SEQS)

    @pl.when(j == 0)
    def _():
        for sq in seqs:
            for hd in range(H_B):
                rows = slice(sq * n, (sq + 1) * n)
                ql_ref[sq, hd * n:(hd + 1) * n, :] = qlat_ref[rows, hd * KV_LORA:(hd + 1) * KV_LORA]
                qp_ref[sq, hd * n:(hd + 1) * n, :] = qpe_ref[rows, hd * 128:(hd + 1) * 128]
        m_ref[...] = jnp.full(m_ref.shape, NEG_INF, F32)
        l_ref[...] = jnp.zeros(l_ref.shape, F32)
        acc_ref[...] = jnp.zeros(acc_ref.shape, F32)

    def attend(ckv_of, pe_scores_of):
        ckvs = [ckv_of(sq).astype(BF16) for sq in seqs]
        scores = [_dot_nt(ql_ref[sq], ckvs[sq]) + pe_scores_of(sq, qp_ref[sq, :, :ROPE_DIM]) for sq in seqs]
        for sq in seqs:
            _dec_update(scores[sq], ckvs[sq], m_ref, l_ref, acc_ref, sq)

    attend(lambda sq: ckv_ref[sq], lambda sq, qp: _dot(qp, kpet_ref[sq].astype(BF16)))

    @pl.when(j == pl.num_programs(1) - 1)
    def _():
        attend(lambda sq: ckvn_ref[sq * n:(sq + 1) * n, :],
               lambda sq, qp: _dot_nt(qp, kpen_ref[sq * n:(sq + 1) * n, :].astype(BF16)))
        for sq in seqs:
            o = (acc_ref[sq] / l_ref[sq]).astype(BF16)
            for hd in range(H_B):
                o_ref[sq * n:(sq + 1) * n, hd * V_DIM:(hd + 1) * V_DIM] = _dot(
                    o[hd * n:(hd + 1) * n], wuv_ref[hd]).astype(o_ref.dtype)


def _dec_mla(qlat, qpe, cache_ckv, cache_kpe_t, ckv_new, kpe_new, wuv_h, b, n):
    past = cache_ckv.shape[1]
    tk = DEC_MLA_KEYS
    sq = DEC_SEQS
    return pl.pallas_call(
        functools.partial(_dec_mla_kernel, n),
        grid=(b // sq, past // tk),
        in_specs=[
            pl.BlockSpec((sq * n, H_B * KV_LORA), lambda bi, j: (bi, 0)),
            pl.BlockSpec((sq * n, H_B * 128), lambda bi, j: (bi, 0)),
            pl.BlockSpec((sq, tk, KV_LORA), lambda bi, j: (bi, j, 0)),
            pl.BlockSpec((sq, ROPE_DIM, tk), lambda bi, j: (bi, 0, j)),
            pl.BlockSpec((sq * n, KV_LORA), lambda bi, j: (bi, 0)),
            pl.BlockSpec((sq * n, ROPE_DIM), lambda bi, j: (bi, 0)),
            pl.BlockSpec((H_B, KV_LORA, V_DIM), lambda bi, j: (0, 0, 0)),
        ],
        out_specs=pl.BlockSpec((sq * n, W_B), lambda bi, j: (bi, 0)),
        out_shape=jax.ShapeDtypeStruct((b * n, W_B), BF16),
        scratch_shapes=[
            pltpu.VMEM((sq, H_B * n, KV_LORA), BF16),
            pltpu.VMEM((sq, H_B * n, 128), BF16),
            pltpu.VMEM((sq, H_B * n, 1), F32),
            pltpu.VMEM((sq, H_B * n, 1), F32),
            pltpu.VMEM((sq, H_B * n, KV_LORA), F32),
        ],
        compiler_params=_params("arbitrary", "arbitrary"),
        name="dec_mla",
    )(qlat, qpe, cache_ckv, cache_kpe_t, ckv_new, kpe_new, wuv_h)


def _out_proj_kernel(oa_ref, ob_ref, x_ref, shift_ref, scale_ref, gate_ref, gn_ref, wgate_ref, w_ref,
                     o_ref):
    nb = x_ref.shape[0]
    for _, rs, ts in _token_chains(x_ref):
        sg = _silu(_dot_nt(_prenorm_tokens(x_ref, rs, gn_ref, scale_ref, shift_ref), wgate_ref[...]))
        oa = (oa_ref[ts, :].astype(F32) * sg[:, :W_A]).astype(BF16)
        ob = (ob_ref[ts, :].astype(F32) * sg[:, W_A:]).astype(BF16)
        y = _dot(oa, w_ref[:W_A, :]) + _dot(ob, w_ref[W_A:, :])
        o_ref[:, rs, :] = x_ref[:, rs, :] + gate_ref[...] * y.reshape(nb, -1, D_MODEL)


def _pool_kernel(tiles_per_seq, pos0, carry, x_ref, shift_ref, scale_ref, gate_ref, gn_ref, pre_ref,
                 win_ref, wgrp_ref, sp_ref, wout_ref, gf_ref, y_ref, st_ref, uext_ref):
    nb, rows, _ = x_ref.shape
    t = pl.program_id(0) % tiles_per_seq
    if carry:
        @pl.when(t == 0)
        def _():
            uext_ref[:, :POOL_PREFIX, :] = pre_ref[...]
    else:
        uext_ref[:, :POOL_PREFIX, :] = pre_ref[...]

    for r0, r1 in _row_chains(rows):
        n = r1 - r0
        x = x_ref[:, r0:r1, :]
        h = _prenorm(x, gn_ref[...], scale_ref[...], shift_ref[...])
        z = _dot(h.reshape(nb * n, D_MODEL).astype(BF16), win_ref[...])
        u = z[:, :W_C].reshape(nb, n, W_C)
        sgate = _silu(z[:, W_C:])
        uext_ref[:, POOL_PREFIX + r0:POOL_PREFIX + r1, :] = u
        pos = pos0 + t * rows + r0 + lax.broadcasted_iota(jnp.int32, (1, n, 1), 1)
        parts = []
        for g, w in enumerate(POOL_WINDOWS):
            cols = slice(g * POOL_GW, (g + 1) * POOL_GW)
            ws = u[:, :, cols]
            for k in range(1, w):
                ws = ws + uext_ref[:, POOL_PREFIX + r0 - k:POOL_PREFIX + r1 - k, cols]
            cnt = jnp.minimum(pos + 1, w).astype(F32)
            dgrp = (ws / cnt - u[:, :, cols]).reshape(nb * n, POOL_GW).astype(BF16)
            parts.append(_dot(dgrp, wgrp_ref[g]))
        m = jnp.concatenate(parts, axis=1) * sp_ref[...]
        y = _dot((m * sgate).astype(BF16), wout_ref[...])
        x2 = x + gate_ref[...] * y.reshape(nb, n, D_MODEL)
        y_ref[:, r0:r1, :] = _rms(x2, EPS) * gf_ref[...]
    tail = uext_ref[:, rows:rows + POOL_PREFIX, :]
    st_ref[...] = tail
    if carry:
        uext_ref[:, :POOL_PREFIX, :] = tail


def _rot_half_cols(w):
    half = ROPE_DIM // 2
    return jnp.concatenate([-w[..., half:], w[..., :half]], axis=-1)


def _rope_tables(start, count, repeat=1):
    half = ROPE_DIM // 2
    inv = ROPE_THETA ** (-np.arange(half, dtype=np.float64) / half)
    ang = np.arange(start, start + count, dtype=np.float64)[:, None] * inv[None, :]
    zero = np.zeros((count, 128 - ROPE_DIM))
    tabs = [np.tile(np.concatenate([t, t, zero], axis=1), (repeat, 1)) for t in (np.cos(ang), np.sin(ang))]
    return tuple(jnp.asarray(t, F32) for t in tabs)


def _group(x3, mod, cos, sin, sample, cache, w, tm_att, tm_pool):
    bt, st, _ = x3.shape
    tokens = bt * st
    lam_init = _lambda_init(0)

    if sample:
        nb, rows, tps = SAMPLE_TILE_SEQS, st, 1
        n_tiles = bt // nb
        x_spec = pl.BlockSpec((nb, rows, D_MODEL), lambda i: (i, 0, 0))

        def mod_spec(layer, comp):
            blk = (layer * 3 + comp) * MOD_ROWS // nb
            return pl.BlockSpec((nb, 1, D_MODEL), lambda i: (blk + i, 0, 0))
    else:
        nb, rows = 1, tm_att
        tps = st // rows
        n_tiles = bt * tps
        x_spec = pl.BlockSpec((1, rows, D_MODEL), lambda i: (i // tps, i % tps, 0))

        def mod_spec(layer, comp):
            base = (layer * 3 + comp) * MOD_ROWS + 16
            return pl.BlockSpec((1, 1, D_MODEL), lambda i: (base + i // tps, 0, 0))

    tm = nb * rows

    def tok(width):
        return pl.BlockSpec((tm, width), lambda i: (i, 0))

    def tok_shape(width, dt):
        return jax.ShapeDtypeStruct((tokens, width), dt)

    vec = lambda width: pl.BlockSpec((1, width), lambda i: (0, 0))
    if sample:
        tab = pl.BlockSpec((tm, 128), lambda i: (0, 0))
    else:
        tab = pl.BlockSpec((tm, 128), lambda i: (i % tps, 0))

    heads_spec = pl.BlockSpec((tm * H_A, DV_A), lambda i: (i, 0))
    heads_shape = jax.ShapeDtypeStruct((tokens * H_A, DV_A), F32)
    vt_rows = H_A * (DV_A + ONES_ROWS)
    key_tile = min(tm, CHAIN_ROWS)
    vt_spec = pl.BlockSpec((1, tm // key_tile, vt_rows, key_tile), lambda i: (i // tps, i % tps, 0, 0))
    vt_shape = jax.ShapeDtypeStruct((bt, st // key_tile, vt_rows, key_tile), BF16)
    qkv_out = pl.pallas_call(
        _in_qkv_kernel,
        grid=(n_tiles,),
        in_specs=[x_spec, mod_spec(0, 0), mod_spec(0, 1), vec(D_MODEL), _resident_rows(0, 3 * W_A, D_MODEL)],
        out_specs=[tok(W_A), heads_spec, heads_spec] + ([] if sample else [tok(W_A), vt_spec]),
        out_shape=[tok_shape(W_A, BF16), heads_shape, heads_shape]
        + ([] if sample else [tok_shape(W_A, BF16), vt_shape]),
        compiler_params=_params("arbitrary"),
        name="in_qkv",
    )(x3, mod, mod, w["g_norm0"], w["w_in_t"])
    qa, k32, v32 = qkv_out[:3]

    lat_in = [x_spec, mod_spec(0, 0), mod_spec(0, 1), vec(D_MODEL), tab, tab,
              _resident_rows(3 * W_A, Q_LORA + KV_LORA, D_MODEL), _resident((256, D_MODEL)),
              vec(Q_LORA), vec(KV_LORA),
              _resident((Q_LORA, H_B * 128)), _resident((Q_LORA, H_B * 128)), _resident((Q_LORA, H_B * 128))]
    lat_args = [x3, mod, mod, w["g_norm0"], cos, sin, w["w_in_t"], w["wkpe"], w["g_qa"], w["g_kva"],
                w["wuq"], w["wuqr"], w["wuqrr"]]
    lam_refs = (w["lam_q1"], w["lam_k1"], w["lam_q2"], w["lam_k2"])

    if sample:
        ckv32, kpe32, qlat, qpe = pl.pallas_call(
            _in_lat_sample_kernel,
            grid=(n_tiles,),
            in_specs=lat_in + [_resident((H_B, 128, KV_LORA))],
            out_specs=[tok(KV_LORA), tok(ROPE_DIM), tok(H_B * KV_LORA), tok(H_B * 128)],
            out_shape=[tok_shape(KV_LORA, F32), tok_shape(ROPE_DIM, F32),
                       tok_shape(H_B * KV_LORA, BF16), tok_shape(H_B * 128, BF16)],
            compiler_params=_params("arbitrary"),
            name="in_lat_sample",
        )(*lat_args, w["wukt"])
        cache_k, cache_v, cache_ckv, cache_kpe = cache
        o_a = _dec_diff(qa, cache_k, cache_v, k32, v32, lam_refs, w["g_subln"], lam_init, bt, st)
        o_b = _dec_mla(qlat, qpe, cache_ckv, cache_kpe, ckv32, kpe32, w["wuv_h"], bt, st)
    else:
        ckv32, kpe32, qcat, kcat, vmla = pl.pallas_call(
            _in_lat_prompt_kernel,
            grid=(n_tiles,),
            in_specs=lat_in + [_resident((KV_LORA, H_B * 128)), _resident((KV_LORA, H_B * 128))],
            out_specs=[tok(KV_LORA), tok(ROPE_DIM), tok(H_B * 256), tok(H_B * 256), vt_spec],
            out_shape=[tok_shape(KV_LORA, F32), tok_shape(ROPE_DIM, F32),
                       tok_shape(H_B * 256, BF16), tok_shape(H_B * 256, BF16), vt_shape],
            compiler_params=_params("arbitrary"),
            name="in_lat_prompt",
        )(*lat_args, w["wuk"], w["wuv"])
        k16, v16t = qkv_out[3:]
        o_a = _flash_diff(qa.reshape(bt, st, W_A), k16.reshape(bt, st, W_A), v16t, lam_refs,
                          w["g_subln"].reshape(DV_A, 1), lam_init, bt, st, key_tile).reshape(tokens, W_A)
        o_b = _flash_mla(qcat.reshape(bt, st, H_B * 256), kcat.reshape(bt, st, H_B * 256),
                         vmla, bt, st, key_tile).reshape(tokens, W_B)

    x1 = pl.pallas_call(
        _out_proj_kernel,
        grid=(n_tiles,),
        in_specs=[tok(W_A), tok(W_B), x_spec, mod_spec(0, 0), mod_spec(0, 1), mod_spec(0, 2), vec(D_MODEL),
                  _resident_rows(4 * W_A + ROPE_DIM, W_ATT, D_MODEL), _resident((W_ATT, D_MODEL))],
        out_specs=x_spec,
        out_shape=jax.ShapeDtypeStruct(x3.shape, F32),
        compiler_params=_params("arbitrary"),
        name="out_proj",
    )(o_a, o_b, x3, mod, mod, mod, w["g_norm0"], w["w_in_t"], w["wout_att"])

    if sample:
        p_nb, p_rows, p_tiles, p_tps = nb, st, n_tiles, 1
        px_spec = x_spec
        pmod = mod_spec
        pre_spec = pl.BlockSpec((nb, POOL_PREFIX, W_C), lambda i: (i, 0, 0))
        st_spec = pl.BlockSpec((nb, POOL_PREFIX, W_C), lambda i: (i, 0, 0))
        pos0 = cache[2].shape[1]
        prefix = w["pool_prefix"]
    else:
        p_nb, p_rows = 1, tm_pool
        p_tps = st // p_rows
        p_tiles = bt * p_tps
        px_spec = pl.BlockSpec((1, p_rows, D_MODEL), lambda i: (i // p_tps, i % p_tps, 0))

        def pmod(layer, comp):
            base = (layer * 3 + comp) * MOD_ROWS + 16
            return pl.BlockSpec((1, 1, D_MODEL), lambda i: (base + i // p_tps, 0, 0))

        pre_spec = pl.BlockSpec((1, POOL_PREFIX, W_C), lambda i: (0, 0, 0))
        st_spec = pl.BlockSpec((1, POOL_PREFIX, W_C), lambda i: (i // p_tps, 0, 0))
        pos0 = 0
        prefix = jnp.zeros((1, POOL_PREFIX, W_C), F32)

    y, state = pl.pallas_call(
        functools.partial(_pool_kernel, p_tps, pos0, not sample),
        grid=(p_tiles,),
        in_specs=[px_spec, pmod(1, 0), pmod(1, 1), pmod(1, 2), vec(D_MODEL), pre_spec,
                  _resident((D_MODEL, 2 * W_C)), _resident((N_POOL_GROUPS, POOL_GW, POOL_GW)),
                  vec(W_C), _resident((W_C, D_MODEL)), vec(D_MODEL)],
        out_specs=[px_spec, st_spec],
        out_shape=[jax.ShapeDtypeStruct(x3.shape, F32),
                   jax.ShapeDtypeStruct((bt, POOL_PREFIX, W_C), F32)],
        scratch_shapes=[pltpu.VMEM((p_nb, POOL_PREFIX + p_rows, W_C), F32)],
        compiler_params=_params("arbitrary"),
        name="pool",
    )(x1, mod, mod, mod, w["g_norm1"], prefix, w["win_pool"], w["wgrp"], w["scale_pool"],
      w["wout_pool"], w["g_final"])

    return (y,
            k32.reshape(1, bt, st, H_A, DV_A), v32.reshape(1, bt, st, H_A, DV_A),
            ckv32.reshape(1, bt, st, KV_LORA), kpe32.reshape(1, bt, st, ROPE_DIM),
            state[None, :, 1:, :])


def kernel(x_prompt, x_sample, c_prompt, c_sample, cache_diff_k, cache_diff_v, cache_mla_ckv, cache_mla_kpe, state_pool, g_norm, w_ada, b_ada, w_in_att, w_out_att, lam_q1, lam_k1, lam_q2, lam_k2, g_subln, g_qa, w_uq, w_uqr, w_uk, w_uv, g_kva, w_in_pool, w_grp_pool, scale_pool, w_out_pool, g_final):
    b_p, s_p, _ = x_prompt.shape
    b_s, n_s, _ = x_sample.shape
    past = cache_diff_k.shape[2]

    c_all = jnp.concatenate(
        [c_sample, c_prompt, jnp.zeros((MOD_ROWS - b_s - b_p, D_MODEL), F32)], axis=0)
    mod = _ada(c_all, w_ada, b_ada)

    w_in_t = w_in_att[0].T
    kpe_w = w_in_t[4 * W_A:4 * W_A + ROPE_DIM].T
    zpad = jnp.zeros((D_MODEL, 128 - ROPE_DIM), F32)
    uqr = w_uqr[0]
    pad_heads = lambda a: jnp.pad(a, ((0, 0), (0, 0), (0, 128 - ROPE_DIM))).reshape(Q_LORA, H_B * 128)
    w = {
        "g_norm0": g_norm[0:1], "g_norm1": g_norm[1:2],
        "w_in_t": w_in_t.astype(BF16),
        "wkpe": jnp.concatenate([kpe_w, zpad, _rot_half_cols(kpe_w), zpad], axis=1).T.astype(BF16),
        "g_qa": g_qa, "g_kva": g_kva,
        "wuq": w_uq[0].reshape(Q_LORA, H_B * NOPE_DIM).astype(BF16),
        "wuqr": pad_heads(uqr).astype(BF16),
        "wuqrr": pad_heads(_rot_half_cols(uqr)).astype(BF16),
        "wuk": w_uk[0].reshape(KV_LORA, H_B * NOPE_DIM).astype(BF16),
        "wuv": w_uv[0].reshape(KV_LORA, H_B * V_DIM).astype(BF16),
        "wukt": jnp.transpose(w_uk[0], (1, 2, 0)).astype(BF16),
        "wuv_h": jnp.transpose(w_uv[0], (1, 0, 2)).astype(BF16),
        "wout_att": w_out_att[0].astype(BF16),
        "lam_q1": lam_q1, "lam_k1": lam_k1, "lam_q2": lam_q2, "lam_k2": lam_k2,
        "g_subln": g_subln,
        "win_pool": w_in_pool[0].astype(BF16),
        "wgrp": w_grp_pool[0].astype(BF16),
        "scale_pool": scale_pool,
        "wout_pool": w_out_pool[0].astype(BF16),
        "g_final": g_final.reshape(1, D_MODEL),
        "pool_prefix": jnp.pad(state_pool[0], ((0, 0), (1, 0), (0, 0))),
    }

    cos_p, sin_p = _rope_tables(0, s_p)
    cos_s, sin_s = _rope_tables(past, n_s, SAMPLE_TILE_SEQS)

    out_p = _group(x_prompt, mod, cos_p, sin_p, False, None, w, 2 * CHAIN_ROWS, 2 * CHAIN_ROWS)
    cache = (cache_diff_k[0].reshape(b_s, past * H_A, DV_A), cache_diff_v[0].reshape(b_s, past * H_A, DV_A),
             cache_mla_ckv[0], jnp.swapaxes(cache_mla_kpe[0], 1, 2))
    out_s = _group(x_sample, mod, cos_s, sin_s, True, cache, w, 0, 0)
    return (out_p[0], out_s[0]) + out_p[1:] + out_s[1:]
```
